```python
import math
import jax, jax.numpy as jnp
from jax import lax
import numpy as np

D_MODEL = 1024
BATCH = 32
SEQ = 2048
DEPTH = 4

MEM_LEN = 256
ROPE_THETA = 10000.0
EPS = 1e-6
Q_BLOCK = 128
MAX_POS_OFFSET = 4096

DIFF_HEADS = 8
DIFF_HEAD_DIM = 64
DIFF_V_DIM = 2 * DIFF_HEAD_DIM
DIFF_QK_W = DIFF_HEADS * 2 * DIFF_HEAD_DIM
DIFF_V_W = DIFF_HEADS * DIFF_V_DIM

MLA_HEADS = 8
MLA_Q_LORA = 384
MLA_KV_LORA = 256
MLA_NOPE_DIM = 64
MLA_ROPE_DIM = 32
MLA_V_DIM = 64
MLA_QK_DIM = MLA_NOPE_DIM + MLA_ROPE_DIM
MLA_V_W = MLA_HEADS * MLA_V_DIM

CROSS_HEADS = 4
CROSS_HEAD_DIM = 128
CROSS_W = CROSS_HEADS * CROSS_HEAD_DIM

N_BRANCHES = 3
D_FF = 4 * D_MODEL

IN_SIZES = (DIFF_QK_W, DIFF_QK_W, DIFF_V_W, MLA_Q_LORA, MLA_KV_LORA + MLA_ROPE_DIM, CROSS_W, N_BRANCHES * D_MODEL)
IN_WIDTH = sum(IN_SIZES)

kernel_name = "hybrid_diffattn_mla_memxattn_sqrelu"


def rmsnorm(x, gain):
    xf = x.astype(jnp.float32)
    y = xf * lax.rsqrt(jnp.mean(xf * xf, axis=-1, keepdims=True) + EPS)
    return (y * gain.astype(jnp.float32)).astype(x.dtype)


def rotary_tables(positions, dim):
    inv_freq = ROPE_THETA ** (-jnp.arange(0, dim, 2, dtype=jnp.float32) / dim)
    ang = positions.astype(jnp.float32)[..., None] * inv_freq
    return jnp.cos(ang), jnp.sin(ang)


def rotary(x, cos, sin):
    xf = x.astype(jnp.float32)
    x1, x2 = jnp.split(xf, 2, axis=-1)
    return jnp.concatenate([x1 * cos - x2 * sin, x2 * cos + x1 * sin], axis=-1).astype(x.dtype)


def causal_block_attention(q, k, v, scale):
    B, S, H, M, Dk = q.shape
    nb = S // Q_BLOCK
    qb = q.reshape(B, nb, Q_BLOCK, H, M, Dk).transpose(1, 0, 2, 3, 4, 5)
    key_pos = jnp.arange(S)

    def one_block(args):
        q_blk, blk = args
        s = jnp.einsum('bqhmd,bkhmd->bhmqk', q_blk, k).astype(jnp.float32) * scale
        q_pos = blk * Q_BLOCK + jnp.arange(Q_BLOCK)
        mask = key_pos[None, :] <= q_pos[:, None]
        s = jnp.where(mask, s, -jnp.inf)
        p = jax.nn.softmax(s, axis=-1).astype(v.dtype)
        return jnp.einsum('bhmqk,bkhd->bqhmd', p, v)

    out = lax.map(one_block, (qb, jnp.arange(nb)))
    return out.transpose(1, 0, 2, 3, 4, 5).reshape(B, S, H, M, v.shape[-1])


def diff_attention(q, k, v, q_gain, k_gain, lam_vecs, out_gain, cos, sin, layer_idx):
    B, S, _ = q.shape
    q = q.reshape(B, S, DIFF_HEADS, 2, DIFF_HEAD_DIM)
    k = k.reshape(B, S, DIFF_HEADS, 2, DIFF_HEAD_DIM)
    v = v.reshape(B, S, DIFF_HEADS, DIFF_V_DIM)
    c, s_ = cos[:, :, None, None, :], sin[:, :, None, None, :]
    q = rotary(rmsnorm(q, q_gain), c, s_)
    k = rotary(rmsnorm(k, k_gain), c, s_)
    o = causal_block_attention(q, k, v, DIFF_HEAD_DIM ** -0.5)
    lam_init = 0.8 - 0.6 * math.exp(-0.3 * layer_idx)
    lv = lam_vecs.astype(jnp.float32)
    lam = jnp.exp(jnp.sum(lv[0] * lv[1])) - jnp.exp(jnp.sum(lv[2] * lv[3])) + lam_init
    o = o[:, :, :, 0, :] - lam.astype(o.dtype) * o[:, :, :, 1, :]
    o = rmsnorm(o, out_gain) * (1.0 - lam_init)
    return o.reshape(B, S, DIFF_V_W)


def latent_attention(c_q, c_kv_kr, qa_gain, w_qb, kva_gain, w_kvb, q_gain, k_gain, cos, sin):
    B, S, _ = c_q.shape
    q = (rmsnorm(c_q, qa_gain) @ w_qb).reshape(B, S, MLA_HEADS, MLA_QK_DIM)
    q_nope, q_rope = jnp.split(q, [MLA_NOPE_DIM], axis=-1)
    q_rope = rotary(q_rope, cos[:, :, None, :], sin[:, :, None, :])
    c_kv, k_rope = jnp.split(c_kv_kr, [MLA_KV_LORA], axis=-1)
    kv = (rmsnorm(c_kv, kva_gain) @ w_kvb).reshape(B, S, MLA_HEADS, MLA_NOPE_DIM + MLA_V_DIM)
    k_nope, v = jnp.split(kv, [MLA_NOPE_DIM], axis=-1)
    k_rope = rotary(k_rope, cos, sin)
    k_rope = jnp.broadcast_to(k_rope[:, :, None, :], (B, S, MLA_HEADS, MLA_ROPE_DIM))
    q = rmsnorm(jnp.concatenate([q_nope, q_rope], axis=-1), q_gain)
    k = rmsnorm(jnp.concatenate([k_nope, k_rope], axis=-1), k_gain)
    o = causal_block_attention(q[:, :, :, None, :], k[:, :, :, None, :], v, MLA_QK_DIM ** -0.5)
    return o.reshape(B, S, MLA_V_W)


def memory_cross_attention(q, mem_n, w_mem_kv, q_gain, k_gain):
    B, S, _ = q.shape
    q = rmsnorm(q.reshape(B, S, CROSS_HEADS, CROSS_HEAD_DIM), q_gain)
    kv = (mem_n @ w_mem_kv).reshape(B, mem_n.shape[1], 2, CROSS_HEADS, CROSS_HEAD_DIM)
    k = rmsnorm(kv[:, :, 0], k_gain)
    v = kv[:, :, 1]
    s = jnp.einsum('bshd,bmhd->bhsm', q, k).astype(jnp.float32) * (CROSS_HEAD_DIM ** -0.5)
    p = jax.nn.softmax(s, axis=-1).astype(v.dtype)
    o = jnp.einsum('bhsm,bmhd->bshd', p, v)
    return o.reshape(B, S, CROSS_W)


def setup_inputs(seed: int = 0) -> dict:
    key = jax.random.key(seed)
    ks = iter(jax.random.split(key, 40))
    L = DEPTH

    def nrm(shape, fan_in, scale=1.0):
        return jax.random.normal(next(ks), shape, jnp.float32) * (scale * fan_in ** -0.5)

    def gain(shape):
        return 1.0 + 0.02 * jax.random.normal(next(ks), shape, jnp.float32)

    res_scale = (2 * DEPTH) ** -0.5
    x = jax.random.normal(next(ks), (BATCH, SEQ, D_MODEL), jnp.float32)
    mem = jax.random.normal(next(ks), (BATCH, MEM_LEN, D_MODEL), jnp.float32)
    offset = jax.random.randint(next(ks), (BATCH, 1), 0, MAX_POS_OFFSET, dtype=jnp.int32)
    positions = (offset + jnp.arange(SEQ, dtype=jnp.int32)[None, :]).astype(jnp.int32)
    return {
        "x": x,
        "mem": mem,
        "positions": positions,
        "g_mix": gain((L, D_MODEL)),
        "w_in": nrm((L, D_MODEL, IN_WIDTH), D_MODEL),
        "b_gate": 0.01 * jax.random.normal(next(ks), (L, N_BRANCHES * D_MODEL), jnp.float32),
        "diff_q_gain": gain((L, DIFF_HEAD_DIM)),
        "diff_k_gain": gain((L, DIFF_HEAD_DIM)),
        "diff_lambda": 0.1 * jax.random.normal(next(ks), (L, 4, DIFF_HEAD_DIM), jnp.float32),
        "diff_out_gain": gain((L, DIFF_V_DIM)),
        "w_diff_o": nrm((L, DIFF_V_W, D_MODEL), DIFF_V_W),
        "mla_qa_gain": gain((L, MLA_Q_LORA)),
        "w_mla_qb": nrm((L, MLA_Q_LORA, MLA_HEADS * MLA_QK_DIM), MLA_Q_LORA),
        "mla_kva_gain": gain((L, MLA_KV_LORA)),
        "w_mla_kvb": nrm((L, MLA_KV_LORA, MLA_HEADS * (MLA_NOPE_DIM + MLA_V_DIM)), MLA_KV_LORA),
        "mla_q_gain": gain((L, MLA_QK_DIM)),
        "mla_k_gain": gain((L, MLA_QK_DIM)),
        "w_mla_o": nrm((L, MLA_V_W, D_MODEL), MLA_V_W),
        "mem_gain": gain((L, D_MODEL)),
        "w_mem_kv": nrm((L, D_MODEL, 2 * CROSS_W), D_MODEL),
        "cross_q_gain": gain((L, CROSS_HEAD_DIM)),
        "cross_k_gain": gain((L, CROSS_HEAD_DIM)),
        "w_cross_o": nrm((L, CROSS_W, D_MODEL), CROSS_W),
        "w_out": nrm((L, D_MODEL, D_MODEL), D_MODEL, res_scale),
        "g_mlp": gain((L, D_MODEL)),
        "w_mlp1": nrm((L, D_MODEL, D_FF), D_MODEL),
        "w_mlp2": nrm((L, D_FF, D_MODEL), D_FF, res_scale),
    }


def reference(x, mem, positions, g_mix, w_in, b_gate, diff_q_gain, diff_k_gain, diff_lambda,
              diff_out_gain, w_diff_o, mla_qa_gain, w_mla_qb, mla_kva_gain, w_mla_kvb,
              mla_q_gain, mla_k_gain, w_mla_o, mem_gain, w_mem_kv, cross_q_gain, cross_k_gain,
              w_cross_o, w_out, g_mlp, w_mlp1, w_mlp2):
    B, S, D = x.shape
    cos_d, sin_d = rotary_tables(positions, DIFF_HEAD_DIM)
    cos_r, sin_r = rotary_tables(positions, MLA_ROPE_DIM)
    split_idx = np.cumsum(IN_SIZES)[:-1].tolist()

    for l in range(DEPTH):
        h = rmsnorm(x, g_mix[l])
        dq, dk, dv, c_q, c_kv_kr, xq, gate_logits = jnp.split(h @ w_in[l], split_idx, axis=-1)

        y_diff = diff_attention(dq, dk, dv, diff_q_gain[l], diff_k_gain[l], diff_lambda[l],
                                diff_out_gain[l], cos_d, sin_d, l) @ w_diff_o[l]
        y_mla = latent_attention(c_q, c_kv_kr, mla_qa_gain[l], w_mla_qb[l], mla_kva_gain[l],
                                 w_mla_kvb[l], mla_q_gain[l], mla_k_gain[l], cos_r, sin_r) @ w_mla_o[l]
        mem_n = rmsnorm(mem, mem_gain[l])
        y_mem = memory_cross_attention(xq, mem_n, w_mem_kv[l], cross_q_gain[l], cross_k_gain[l]) @ w_cross_o[l]

        gates = jax.nn.sigmoid((gate_logits + b_gate[l]).reshape(B, S, N_BRANCHES, D))
        merged = gates[:, :, 0] * y_diff + gates[:, :, 1] * y_mla + gates[:, :, 2] * y_mem
        x = x + merged @ w_out[l]

        u = rmsnorm(x, g_mlp[l]) @ w_mlp1[l]
        x = x + jnp.square(jax.nn.relu(u)) @ w_mlp2[l]
    return x
```

```python
import functools
import math

import jax
import jax.numpy as jnp
from jax import lax
from jax.experimental import pallas as pl
from jax.experimental.pallas import tpu as pltpu

D_MODEL = 1024
DEPTH = 4
MEM_LEN = 256
ROPE_THETA = 10000.0
EPS = 1e-6

DIFF_HEADS = 8
DIFF_HEAD_DIM = 64
MLA_HEADS = 8
MLA_Q_LORA = 384
MLA_KV_LORA = 256
MLA_NOPE_DIM = 64
MLA_ROPE_DIM = 32
MLA_V_DIM = 64
MLA_QK_DIM = MLA_NOPE_DIM + MLA_ROPE_DIM
CROSS_HEADS = 4
CROSS_HEAD_DIM = 128
D_FF = 4 * D_MODEL

LANES = 128
HALF_LANES = LANES // 2

C_DQ = 0
C_DK = C_DQ + DIFF_HEADS * LANES
C_DV = C_DK + DIFF_HEADS * LANES
C_CQ = C_DV + DIFF_HEADS * LANES
C_CKV = C_CQ + MLA_Q_LORA
C_KR = C_CKV + MLA_KV_LORA
C_XQ = C_KR + LANES
C_GATE = C_XQ + CROSS_HEADS * CROSS_HEAD_DIM
IN_WIDTH_PACKED = C_GATE + 3 * D_MODEL

TM_IN = 256
TM_OUT = 512
TQ = 256
TQ_CROSS = 512
VMEM_LIMIT = 56 * 1024 * 1024

BF16 = jnp.bfloat16
F32 = jnp.float32


def _const_spec(shape):
    nd = len(shape)
    return pl.BlockSpec(shape, lambda *_: (0,) * nd, pipeline_mode=pl.Buffered(1))


def _dot(a, b):
    return jnp.dot(a, b, preferred_element_type=F32)


def _dot_nt(a, b):
    return lax.dot_general(a, b, (((1,), (1,)), ((), ())), preferred_element_type=F32)


def _inv_rms(y, n):
    return lax.rsqrt(jnp.sum(y * y, axis=-1, keepdims=True) * (1.0 / n) + EPS)


def _rope64(y, cos, sin):
    return y * cos + pltpu.roll(y, HALF_LANES, axis=1) * sin


def _rope_table_kernel(pos_ref, invf_ref, cos_ref, sin_ref):
    ang = pos_ref[...].astype(F32) * invf_ref[...]
    cos_ref[...] = jnp.cos(ang)
    sin_ref[...] = jnp.sin(ang)


def _rope_tables(positions):
    nfreq = DIFF_HEAD_DIM // 2
    per_row = LANES // nfreq
    t = positions.size
    inv_freq = ROPE_THETA ** (-jnp.arange(0, DIFF_HEAD_DIM, 2, dtype=F32) / DIFF_HEAD_DIM)
    pos_rep = jnp.repeat(positions.reshape(t // per_row, per_row), nfreq, axis=1)
    invf = jnp.tile(inv_freq, per_row)[None, :]
    rows = t // per_row
    tr = 2048
    cos, sin = pl.pallas_call(
        _rope_table_kernel,
        grid=(rows // tr,),
        in_specs=[pl.BlockSpec((tr, LANES), lambda i: (i, 0)),
                  pl.BlockSpec((1, LANES), lambda i: (0, 0))],
        out_specs=[pl.BlockSpec((tr, LANES), lambda i: (i, 0))] * 2,
        out_shape=[jax.ShapeDtypeStruct((rows, LANES), F32)] * 2,
        name="rope_tables",
    )(pos_rep, invf)
    return cos.reshape(t, nfreq), sin.reshape(t, nfreq)


def _inproj_kernel(x_ref, cosd_ref, sind_ref, cosm_ref, sinm_ref,
                   gmix_ref, win_ref, bgate_ref, dqg_ref, dkg_ref,
                   qag_ref, wqb_ref, kvag_ref, wkvk_ref, wkvv_ref, mqg_ref, mkg_ref, xqg_ref,
                   dq_ref, dk_ref, dv_ref, mq_ref, mk_ref, mv_ref, xq_ref, gate_ref):
    x = x_ref[...]
    h = (x * _inv_rms(x, D_MODEL) * gmix_ref[...]).astype(BF16)
    tm = x.shape[0]

    cosd, sind = cosd_ref[...], sind_ref[...]
    cosm, sinm = cosm_ref[...], sinm_ref[...]
    lane = lax.broadcasted_iota(jnp.int32, (tm, LANES), 1)
    is_map0 = (lane & 32) == 0

    def diff_qk(col0, gain_ref, out_ref):
        y = _dot(h, win_ref[:, col0:col0 + DIFF_HEADS * LANES])
        gain = gain_ref[...]
        for hd in range(DIFF_HEADS):
            yh = y[:, hd * LANES:(hd + 1) * LANES]
            sq = yh * yh
            s_all = jnp.sum(sq, axis=-1, keepdims=True)
            s0 = jnp.sum(jnp.where(is_map0, sq, 0.0), axis=-1, keepdims=True)
            r0 = lax.rsqrt(s0 * (1.0 / DIFF_HEAD_DIM) + EPS)
            r1 = lax.rsqrt((s_all - s0) * (1.0 / DIFF_HEAD_DIM) + EPS)
            yn = yh * jnp.where(is_map0, r0, r1) * gain
            out_ref[:, hd * LANES:(hd + 1) * LANES] = _rope64(yn, cosd, sind).astype(BF16)

    diff_qk(C_DQ, dqg_ref, dq_ref)
    diff_qk(C_DK, dkg_ref, dk_ref)
    dv_ref[...] = _dot(h, win_ref[:, C_DV:C_DV + DIFF_HEADS * LANES]).astype(BF16)

    lat = _dot(h, win_ref[:, C_CQ:C_XQ])
    cq = lat[:, 0:MLA_Q_LORA]
    ckv = lat[:, MLA_Q_LORA:MLA_Q_LORA + MLA_KV_LORA]
    kr = lat[:, MLA_Q_LORA + MLA_KV_LORA:]
    cqn = (cq * _inv_rms(cq, MLA_Q_LORA) * qag_ref[...]).astype(BF16)
    ckvn = (ckv * _inv_rms(ckv, MLA_KV_LORA) * kvag_ref[...]).astype(BF16)
    kr = _rope64(kr, cosm, sinm)

    q = _dot(cqn, wqb_ref[...])
    kn = _dot(ckvn, wkvk_ref[...])
    mv_ref[...] = _dot(ckvn, wkvv_ref[...]).astype(BF16)
    mqg, mkg = mqg_ref[...], mkg_ref[...]
    for hd in range(MLA_HEADS):
        sl = slice(hd * LANES, (hd + 1) * LANES)
        qh = _rope64(q[:, sl], cosm, sinm)
        mq_ref[:, sl] = (qh * _inv_rms(qh, MLA_QK_DIM) * mqg).astype(BF16)
        kh = kn[:, sl] + kr
        mk_ref[:, sl] = (kh * _inv_rms(kh, MLA_QK_DIM) * mkg).astype(BF16)

    xq = _dot(h, win_ref[:, C_XQ:C_GATE])
    xqg = xqg_ref[...]
    for hd in range(CROSS_HEADS):
        sl = slice(hd * LANES, (hd + 1) * LANES)
        yh = xq[:, sl]
        xq_ref[:, sl] = (yh * _inv_rms(yh, CROSS_HEAD_DIM) * xqg).astype(BF16)

    for br in range(3):
        sl = slice(br * D_MODEL, (br + 1) * D_MODEL)
        z = _dot(h, win_ref[:, C_GATE + br * D_MODEL:C_GATE + (br + 1) * D_MODEL]) + bgate_ref[:, sl]
        gate_ref[:, sl] = (1.0 / (1.0 + jnp.exp(-z))).astype(BF16)


def _inproj(x, tabs, p):
    t = x.shape[0]
    tm = TM_IN
    row = lambda w: pl.BlockSpec((tm, w), lambda i: (i, 0))
    consts = [p["g_mix"], p["w_in"], p["b_gate"], p["dq_gain"], p["dk_gain"],
              p["qa_gain"], p["w_qb"], p["kva_gain"], p["w_kvk"], p["w_kvv"],
              p["mq_gain"], p["mk_gain"], p["xq_gain"]]
    out_widths = [DIFF_HEADS * LANES] * 3 + [MLA_HEADS * LANES] * 2 + [MLA_HEADS * MLA_V_DIM,
                                                                      CROSS_HEADS * CROSS_HEAD_DIM, 3 * D_MODEL]
    return pl.pallas_call(
        _inproj_kernel,
        grid=(t // tm,),
        in_specs=[row(D_MODEL)] + [row(LANES)] * 4 + [_const_spec(c.shape) for c in consts],
        out_specs=[row(w) for w in out_widths],
        out_shape=[jax.ShapeDtypeStruct((t, w), BF16) for w in out_widths],
        compiler_params=pltpu.CompilerParams(dimension_semantics=("parallel",),
                                             vmem_limit_bytes=VMEM_LIMIT),
        name="inproj",
    )(x, *tabs, *consts)


def _causal_bias(tq):
    r = lax.broadcasted_iota(jnp.int32, (tq, tq), 0)
    c = lax.broadcasted_iota(jnp.int32, (tq, tq), 1)
    return jnp.where(c <= r, 0.0, -jnp.inf).astype(F32)


def _softmax_pv(s_past, s_diag, v_past, v_diag):
    m = jnp.max(s_diag, axis=-1, keepdims=True)
    if s_past is not None:
        m = jnp.maximum(m, jnp.max(s_past, axis=-1, keepdims=True))
    p_diag = jnp.exp(s_diag - m)
    l = jnp.sum(p_diag, axis=-1, keepdims=True)
    acc = [_dot(p_diag.astype(BF16), v) for v in v_diag]
    if s_past is not None:
        p_past = jnp.exp(s_past - m)
        l = l + jnp.sum(p_past, axis=-1, keepdims=True)
        p_past = p_past.astype(BF16)
        acc = [a + _dot(p_past, v) for a, v in zip(acc, v_past)]
    return acc, l


def _diff_attn_kernel(lam_ref, og_ref, q_ref, k_ref, v_ref, o_ref):
    s_len = q_ref.shape[0]
    tq = TQ
    lane = lax.broadcasted_iota(jnp.int32, (tq, LANES), 1)
    is_map0 = (lane & 32) == 0
    bias = _causal_bias(tq)
    bias2 = jnp.concatenate([bias, bias], axis=0)

    lv = lam_ref[...]
    lam = (jnp.exp(jnp.sum(lv[0:1] * lv[1:2], axis=-1, keepdims=True))
           - jnp.exp(jnp.sum(lv[2:3] * lv[3:4], axis=-1, keepdims=True))
           + lv[4:5, 0:1])
    out_scale = og_ref[...] * (1.0 - lv[4:5, 0:1])

    for qi in range(s_len // tq):
        lo, hi = qi * tq, (qi + 1) * tq
        q = q_ref[lo:hi, :]
        zero = jnp.zeros_like(q)
        qq = jnp.concatenate([jnp.where(is_map0, q, zero), jnp.where(is_map0, zero, q)], axis=0)
        s_diag = _dot_nt(qq, k_ref[lo:hi, :]) + bias2
        s_past = _dot_nt(qq, k_ref[0:lo, :]) if qi > 0 else None
        v_past = [v_ref[0:lo, :]] if qi > 0 else None
        (acc,), l = _softmax_pv(s_past, s_diag, v_past, [v_ref[lo:hi, :]])
        o = acc / l
        o = o[0:tq] - lam * o[tq:2 * tq]
        o_ref[lo:hi, :] = (o * _inv_rms(o, 2 * DIFF_HEAD_DIM) * out_scale).astype(BF16)


def _diff_attn(lam_tab, out_gain, dq, dk, dv, batch, seq):
    blk = pl.BlockSpec((None, seq, LANES), lambda b, h: (b, 0, h))
    shp = (batch, seq, DIFF_HEADS * LANES)
    out = pl.pallas_call(
        _diff_attn_kernel,
        grid=(batch, DIFF_HEADS),
        in_specs=[_const_spec(lam_tab.shape), _const_spec(out_gain.shape), blk, blk, blk],
        out_specs=blk,
        out_shape=jax.ShapeDtypeStruct(shp, BF16),
        compiler_params=pltpu.CompilerParams(dimension_semantics=("parallel", "parallel"),
                                             vmem_limit_bytes=VMEM_LIMIT),
        name="diff_attn",
    )(lam_tab, out_gain, dq.reshape(shp), dk.reshape(shp), dv.reshape(shp))
    return out.reshape(batch * seq, DIFF_HEADS * LANES)


def _mla_attn_kernel(q_ref, k_ref, v_ref, o_ref):
    s_len = q_ref.shape[0]
    tq = TQ
    lane = lax.broadcasted_iota(jnp.int32, (tq, LANES), 1)
    is_head0 = lane < MLA_V_DIM
    bias = _causal_bias(tq)

    def v_halves(lo, hi):
        v = v_ref[lo:hi, :]
        zero = jnp.zeros_like(v)
        lane_v = lax.broadcasted_iota(jnp.int32, v.shape, 1)
        return jnp.where(lane_v < MLA_V_DIM, v, zero), jnp.where(lane_v < MLA_V_DIM, zero, v)

    for qi in range(s_len // tq):
        lo, hi = qi * tq, (qi + 1) * tq
        v_diag = v_halves(lo, hi)
        v_past = v_halves(0, lo) if qi > 0 else (None, None)
        o = None
        for hd in range(2):
            sl = slice(hd * LANES, (hd + 1) * LANES)
            q = q_ref[lo:hi, sl]
            s_diag = _dot_nt(q, k_ref[lo:hi, sl]) + bias
            s_past = _dot_nt(q, k_ref[0:lo, sl]) if qi > 0 else None
            (acc,), l = _softmax_pv(s_past, s_diag, [v_past[hd]], [v_diag[hd]])
            o = acc / l if o is None else o + acc / l
        o_ref[lo:hi, :] = o.astype(BF16)


def _mla_attn(mq, mk, mv, batch, seq):
    qk_blk = pl.BlockSpec((None, seq, 2 * LANES), lambda b, h: (b, 0, h))
    v_blk = pl.BlockSpec((None, seq, 2 * MLA_V_DIM), lambda b, h: (b, 0, h))
    qk_shp = (batch, seq, MLA_HEADS * LANES)
    v_shp = (batch, seq, MLA_HEADS * MLA_V_DIM)
    out = pl.pallas_call(
        _mla_attn_kernel,
        grid=(batch, MLA_HEADS // 2),
        in_specs=[qk_blk, qk_blk, v_blk],
        out_specs=v_blk,
        out_shape=jax.ShapeDtypeStruct(v_shp, BF16),
        compiler_params=pltpu.CompilerParams(dimension_semantics=("parallel", "parallel"),
                                             vmem_limit_bytes=VMEM_LIMIT),
        name="mla_attn",
    )(mq.reshape(qk_shp), mk.reshape(qk_shp), mv.reshape(v_shp))
    return out.reshape(batch * seq, MLA_HEADS * MLA_V_DIM)


def _mem_kv_kernel(mem_ref, mg_ref, w_ref, kg_ref, k_ref, v_ref):
    m = mem_ref[...]
    mn = (m * _inv_rms(m, D_MODEL) * mg_ref[...]).astype(BF16)
    kv = _dot(mn, w_ref[...])
    width = CROSS_HEADS * CROSS_HEAD_DIM
    kg = kg_ref[...]
    for hd in range(CROSS_HEADS):
        sl = slice(hd * LANES, (hd + 1) * LANES)
        kh = kv[:, sl]
        k_ref[:, sl] = (kh * _inv_rms(kh, CROSS_HEAD_DIM) * kg).astype(BF16)
    v_ref[...] = kv[:, width:].astype(BF16)


def _mem_kv(mem2d, p):
    rows = mem2d.shape[0]
    width = CROSS_HEADS * CROSS_HEAD_DIM
    tm = 512
    return pl.pallas_call(
        _mem_kv_kernel,
        grid=(rows // tm,),
        in_specs=[pl.BlockSpec((tm, D_MODEL), lambda i: (i, 0)), _const_spec(p["mem_gain"].shape),
                  _const_spec(p["w_mem_kv"].shape), _const_spec(p["ck_gain"].shape)],
        out_specs=[pl.BlockSpec((tm, width), lambda i: (i, 0))] * 2,
        out_shape=[jax.ShapeDtypeStruct((rows, width), BF16)] * 2,
        compiler_params=pltpu.CompilerParams(dimension_semantics=("parallel",),
                                             vmem_limit_bytes=VMEM_LIMIT),
        name="mem_kv",
    )(mem2d, p["mem_gain"], p["w_mem_kv"], p["ck_gain"])


def _cross_attn_kernel(q_ref, k_ref, v_ref, o_ref):
    for hd in range(CROSS_HEADS):
        sl = slice(hd * LANES, (hd + 1) * LANES)
        s = _dot_nt(q_ref[:, sl], k_ref[:, sl])
        p = jnp.exp(s - jnp.max(s, axis=-1, keepdims=True))
        l = jnp.sum(p, axis=-1, keepdims=True)
        o_ref[:, sl] = (_dot(p.astype(BF16), v_ref[:, sl]) / l).astype(BF16)


def _cross_attn(xq, ck, cv, batch, seq):
    width = CROSS_HEADS * CROSS_HEAD_DIM
    tq = TQ_CROSS
    q_blk = pl.BlockSpec((None, tq, width), lambda b, i: (b, i, 0))
    kv_blk = pl.BlockSpec((None, MEM_LEN, width), lambda b, i: (b, 0, 0))
    out = pl.pallas_call(
        _cross_attn_kernel,
        grid=(batch, seq // tq),
        in_specs=[q_blk, kv_blk, kv_blk],
        out_specs=q_blk,
        out_shape=jax.ShapeDtypeStruct((batch, seq, width), BF16),
        compiler_params=pltpu.CompilerParams(dimension_semantics=("parallel", "parallel"),
                                             vmem_limit_bytes=VMEM_LIMIT),
        name="cross_attn",
    )(xq.reshape(batch, seq, width), ck.reshape(batch, MEM_LEN, width), cv.reshape(batch, MEM_LEN, width))
    return out.reshape(batch * seq, width)


def _outproj_kernel(x_ref, od_ref, om_ref, oc_ref, g_ref, wdo_ref, wmo_ref, wco_ref, wout_ref, o_ref):
    merged = (g_ref[:, 0:D_MODEL].astype(F32) * _dot(od_ref[...], wdo_ref[...])
              + g_ref[:, D_MODEL:2 * D_MODEL].astype(F32) * _dot(om_ref[...], wmo_ref[...])
              + g_ref[:, 2 * D_MODEL:].astype(F32) * _dot(oc_ref[...], wco_ref[...]))
    o_ref[...] = x_ref[...] + _dot(merged.astype(BF16), wout_ref[...])


def _outproj(x, od, om, oc, gates, p):
    t = x.shape[0]
    tm = TM_OUT
    row = lambda w: pl.BlockSpec((tm, w), lambda i: (i, 0))
    consts = [p["w_diff_o"], p["w_mla_o"], p["w_cross_o"], p["w_out"]]
    return pl.pallas_call(
        _outproj_kernel,
        grid=(t // tm,),
        in_specs=[row(D_MODEL), row(od.shape[1]), row(om.shape[1]), row(oc.shape[1]), row(gates.shape[1])]
                 + [_const_spec(c.shape) for c in consts],
        out_specs=row(D_MODEL),
        out_shape=jax.ShapeDtypeStruct((t, D_MODEL), F32),
        compiler_params=pltpu.CompilerParams(dimension_semantics=("parallel",),
                                             vmem_limit_bytes=VMEM_LIMIT),
        name="outproj",
    )(x, od, om, oc, gates, *consts)


FF_CHUNK = 1024


def _mlp_kernel(x_ref, g_ref, w1_ref, w2_ref, o_ref):
    x = x_ref[...]
    h = (x * _inv_rms(x, D_MODEL) * g_ref[...]).astype(BF16)
    y = x
    for c in range(D_FF // FF_CHUNK):
        sl = slice(c * FF_CHUNK, (c + 1) * FF_CHUNK)
        u = jnp.maximum(_dot(h, w1_ref[:, sl]), 0.0)
        y = y + _dot((u * u).astype(BF16), w2_ref[sl, :])
    o_ref[...] = y


def _mlp(x, p):
    t = x.shape[0]
    tm = TM_OUT
    row = pl.BlockSpec((tm, D_MODEL), lambda i: (i, 0))
    consts = [p["g_mlp"], p["w_mlp1"], p["w_mlp2"]]
    return pl.pallas_call(
        _mlp_kernel,
        grid=(t // tm,),
        in_specs=[row] + [_const_spec(c.shape) for c in consts],
        out_specs=row,
        out_shape=jax.ShapeDtypeStruct((t, D_MODEL), F32),
        compiler_params=pltpu.CompilerParams(dimension_semantics=("parallel",),
                                             vmem_limit_bytes=VMEM_LIMIT),
        name="mlp",
    )(x, *consts)


def _pack_layer(l, w_in, b_gate, g_mix, diff_q_gain, diff_k_gain, diff_lambda, diff_out_gain, w_diff_o,
                mla_qa_gain, w_mla_qb, mla_kva_gain, w_mla_kvb, mla_q_gain, mla_k_gain, w_mla_o,
                mem_gain, w_mem_kv, cross_q_gain, cross_k_gain, w_cross_o, w_out, g_mlp, w_mlp1, w_mlp2):
    d = D_MODEL
    half = DIFF_HEAD_DIM // 2
    o_dq, o_dk, o_dv = 0, 1024, 2048
    o_cq = 3072
    o_ckv = o_cq + MLA_Q_LORA
    o_kr = o_ckv + MLA_KV_LORA
    o_xq = o_kr + MLA_ROPE_DIM
    o_gate = o_xq + CROSS_HEADS * CROSS_HEAD_DIM

    def diff_cols(w):
        r = w.shape[0]
        return w.reshape(r, DIFF_HEADS, 2, 2, half).transpose(0, 1, 3, 2, 4).reshape(r, DIFF_HEADS * LANES)

    def diff_gain(g, scale):
        return (jnp.broadcast_to(g.reshape(2, 1, half), (2, 2, half)).reshape(1, LANES) * scale).astype(F32)

    rh = MLA_ROPE_DIM // 2

    def mla_head_layout(nope, rope):
        pad = jnp.zeros(nope.shape[:-1] + (LANES - MLA_QK_DIM,), nope.dtype)
        return jnp.concatenate([rope[..., :rh], nope[..., :HALF_LANES - rh], rope[..., rh:],
                                nope[..., HALF_LANES - rh:], pad], axis=-1)

    kr_w = w_in[:, o_kr:o_xq]
    zpad = jnp.zeros((d, HALF_LANES - rh), F32)
    kr_cols = jnp.concatenate([kr_w[:, :rh], zpad, kr_w[:, rh:], zpad], axis=1)
    w_in_p = jnp.concatenate([diff_cols(w_in[:, o_dq:o_dk]), diff_cols(w_in[:, o_dk:o_dv]),
                              w_in[:, o_dv:o_cq], w_in[:, o_cq:o_kr], kr_cols,
                              w_in[:, o_xq:]], axis=1).astype(BF16)

    wqb = w_mla_qb.reshape(MLA_Q_LORA, MLA_HEADS, MLA_QK_DIM)
    w_qb_p = mla_head_layout(wqb[..., :MLA_NOPE_DIM], wqb[..., MLA_NOPE_DIM:]).reshape(
        MLA_Q_LORA, MLA_HEADS * LANES).astype(BF16)
    wkvb = w_mla_kvb.reshape(MLA_KV_LORA, MLA_HEADS, MLA_NOPE_DIM + MLA_V_DIM)
    k_nope = wkvb[..., :MLA_NOPE_DIM]
    w_kvk_p = mla_head_layout(k_nope, jnp.zeros(k_nope.shape[:-1] + (MLA_ROPE_DIM,), F32)).reshape(
        MLA_KV_LORA, MLA_HEADS * LANES).astype(BF16)
    w_kvv_p = wkvb[..., MLA_NOPE_DIM:].reshape(MLA_KV_LORA, MLA_HEADS * MLA_V_DIM).astype(BF16)

    def mla_gain(g, scale):
        return (mla_head_layout(g[:MLA_NOPE_DIM], g[MLA_NOPE_DIM:]) * scale)[None, :].astype(F32)

    lam_init = 0.8 - 0.6 * math.exp(-0.3 * l)
    lam_tab = jnp.zeros((8, LANES), F32)
    lam_tab = lam_tab.at[0:4, 0:DIFF_HEAD_DIM].set(diff_lambda.astype(F32))
    lam_tab = lam_tab.at[4, :].set(lam_init)

    return dict(
        g_mix=g_mix[None, :], w_in=w_in_p, b_gate=b_gate[None, :],
        dq_gain=diff_gain(diff_q_gain, DIFF_HEAD_DIM ** -0.5), dk_gain=diff_gain(diff_k_gain, 1.0),
        qa_gain=mla_qa_gain[None, :], w_qb=w_qb_p, kva_gain=mla_kva_gain[None, :],
        w_kvk=w_kvk_p, w_kvv=w_kvv_p,
        mq_gain=mla_gain(mla_q_gain, MLA_QK_DIM ** -0.5), mk_gain=mla_gain(mla_k_gain, 1.0),
        xq_gain=(cross_q_gain * CROSS_HEAD_DIM ** -0.5)[None, :],
        lam_tab=lam_tab, out_gain=diff_out_gain[None, :],
        w_diff_o=w_diff_o.astype(BF16), w_mla_o=w_mla_o.astype(BF16),
        mem_gain=mem_gain[None, :], w_mem_kv=w_mem_kv.astype(BF16), ck_gain=cross_k_gain[None, :],
        w_cross_o=w_cross_o.astype(BF16), w_out=w_out.astype(BF16),
        g_mlp=g_mlp[None, :], w_mlp1=w_mlp1.astype(BF16), w_mlp2=w_mlp2.astype(BF16),
    )


def _rope_lane_tables(positions):
    cos, sin = _rope_tables(positions)
    t = cos.shape[0]
    cosd = jnp.tile(cos, (1, 4))
    sind = jnp.concatenate([-sin, -sin, sin, sin], axis=1)
    rh = MLA_ROPE_DIM // 2
    cos_r, sin_r = cos[:, ::2], sin[:, ::2]
    ones = jnp.ones((t, HALF_LANES - rh), F32)
    zeros = jnp.zeros((t, HALF_LANES - rh), F32)
    cosm = jnp.concatenate([cos_r, ones, cos_r, ones], axis=1)
    sinm = jnp.concatenate([-sin_r, zeros, sin_r, zeros], axis=1)
    return cosd, sind, cosm, sinm


def kernel(x, mem, positions, g_mix, w_in, b_gate, diff_q_gain, diff_k_gain, diff_lambda, diff_out_gain,
           w_diff_o, mla_qa_gain, w_mla_qb, mla_kva_gain, w_mla_kvb, mla_q_gain, mla_k_gain, w_mla_o,
           mem_gain, w_mem_kv, cross_q_gain, cross_k_gain, w_cross_o, w_out, g_mlp, w_mlp1, w_mlp2):
    batch, seq, d = x.shape
    tabs = _rope_lane_tables(positions)
    xt = x.reshape(batch * seq, d)
    mem2d = mem.reshape(batch * mem.shape[1], d)
    for l in range(DEPTH):
        p = _pack_layer(l, w_in[l], b_gate[l], g_mix[l], diff_q_gain[l], diff_k_gain[l], diff_lambda[l],
                        diff_out_gain[l], w_diff_o[l], mla_qa_gain[l], w_mla_qb[l], mla_kva_gain[l],
                        w_mla_kvb[l], mla_q_gain[l], mla_k_gain[l], w_mla_o[l], mem_gain[l], w_mem_kv[l],
                        cross_q_gain[l], cross_k_gain[l], w_cross_o[l], w_out[l], g_mlp[l], w_mlp1[l],
                        w_mlp2[l])
        dq, dk, dv, mq, mk, mv, xq, gates = _inproj(xt, tabs, p)
        od = _diff_attn(p["lam_tab"], p["out_gain"], dq, dk, dv, batch, seq)
        om = _mla_attn(mq, mk, mv, batch, seq)
        ck, cv = _mem_kv(mem2d, p)
        oc = _cross_attn(xq, ck, cv, batch, seq)
        xt = _outproj(xt, od, om, oc, gates, p)
        xt = _mlp(xt, p)
    return xt.reshape(batch, seq, d)
```

```python
import math

import jax
import jax.numpy as jnp
from jax import lax
from jax.experimental import pallas as pl
from jax.experimental.pallas import tpu as pltpu

D_MODEL = 1024
DEPTH = 4
MEM_LEN = 256
ROPE_THETA = 10000.0
EPS = 1e-6

DIFF_HEADS = 8
DIFF_HEAD_DIM = 64
MLA_HEADS = 8
MLA_Q_LORA = 384
MLA_KV_LORA = 256
MLA_NOPE_DIM = 64
MLA_ROPE_DIM = 32
MLA_V_DIM = 64
MLA_QK_DIM = MLA_NOPE_DIM + MLA_ROPE_DIM
CROSS_HEADS = 4
CROSS_HEAD_DIM = 128
D_FF = 4 * D_MODEL

LANES = 128
SUBLANES = 8
HALF_LANES = LANES // 2
LOG2E = math.log2(math.e)

C_DQ = 0
C_DK = C_DQ + DIFF_HEADS * LANES
C_DV = C_DK + DIFF_HEADS * LANES
C_CQ = C_DV + DIFF_HEADS * LANES
C_CKV = C_CQ + MLA_Q_LORA
C_KR = C_CKV + MLA_KV_LORA
C_XQ = C_KR + LANES
C_GATE = C_XQ + CROSS_HEADS * CROSS_HEAD_DIM
IN_WIDTH_PACKED = C_GATE + 3 * D_MODEL

TM_IN = 256
TM_OUT = 512
TQ = 256
SM_ROWS = 32
TQ_CROSS = 512
VMEM_LIMIT = 56 * 1024 * 1024

BF16 = jnp.bfloat16
F32 = jnp.float32


def _const_spec(shape):
    nd = len(shape)
    return pl.BlockSpec(shape, lambda *_: (0,) * nd, pipeline_mode=pl.Buffered(1))


def _dot(a, b):
    return jnp.dot(a, b, preferred_element_type=F32)


def _dot_nt(a, b):
    return lax.dot_general(a, b, (((1,), (1,)), ((), ())), preferred_element_type=F32)


def _inv_rms(y, n):
    return lax.rsqrt(jnp.sum(y * y, axis=-1, keepdims=True) * (1.0 / n) + EPS)


def _rope64(y, cos, sin):
    return y * cos + pltpu.roll(y, HALF_LANES, axis=1) * sin


def _rope_table_kernel(pos_ref, invf_ref, cos_ref, sin_ref):
    ang = pos_ref[...].astype(F32) * invf_ref[...]
    cos_ref[...] = jnp.cos(ang)
    sin_ref[...] = jnp.sin(ang)


def _rope_tables(positions):
    nfreq = DIFF_HEAD_DIM // 2
    per_row = LANES // nfreq
    t = positions.size
    inv_freq = ROPE_THETA ** (-jnp.arange(0, DIFF_HEAD_DIM, 2, dtype=F32) / DIFF_HEAD_DIM)
    pos_rep = jnp.repeat(positions.reshape(t // per_row, per_row), nfreq, axis=1)
    invf = jnp.tile(inv_freq, per_row)[None, :]
    rows = t // per_row
    tr = min(2048, rows)
    cos, sin = pl.pallas_call(
        _rope_table_kernel,
        grid=(rows // tr,),
        in_specs=[pl.BlockSpec((tr, LANES), lambda i: (i, 0)),
                  pl.BlockSpec((1, LANES), lambda i: (0, 0))],
        out_specs=[pl.BlockSpec((tr, LANES), lambda i: (i, 0))] * 2,
        out_shape=[jax.ShapeDtypeStruct((rows, LANES), F32)] * 2,
        name="rope_tables",
    )(pos_rep, invf)
    return cos.reshape(t, nfreq), sin.reshape(t, nfreq)


def _inproj_kernel(x_ref, cosd_ref, sind_ref, cosm_ref, sinm_ref,
                   gmix_ref, win_ref, bgate_ref, dqg_ref, dkg_ref,
                   qag_ref, wqb_ref, kvag_ref, wkvk_ref, wkvv_ref, mqg_ref, mkg_ref, xqg_ref,
                   dq_ref, dk_ref, dv_ref, mq_ref, mk_ref, mv_ref, xq_ref, gate_ref):
    x = x_ref[...]
    h = (x * _inv_rms(x, D_MODEL) * gmix_ref[...]).astype(BF16)
    tm = x.shape[0]

    cosd, sind = cosd_ref[...], sind_ref[...]
    cosm, sinm = cosm_ref[...], sinm_ref[...]
    lane = lax.broadcasted_iota(jnp.int32, (tm, LANES), 1)
    is_map0 = (lane & 32) == 0

    def diff_qk(col0, gain_ref, out_ref):
        y = _dot(h, win_ref[:, col0:col0 + DIFF_HEADS * LANES])
        gain = gain_ref[...]
        for hd in range(DIFF_HEADS):
            yh = y[:, hd * LANES:(hd + 1) * LANES]
            sq = yh * yh
            s_all = jnp.sum(sq, axis=-1, keepdims=True)
            s0 = jnp.sum(jnp.where(is_map0, sq, 0.0), axis=-1, keepdims=True)
            r0 = lax.rsqrt(s0 * (1.0 / DIFF_HEAD_DIM) + EPS)
            r1 = lax.rsqrt((s_all - s0) * (1.0 / DIFF_HEAD_DIM) + EPS)
            yn = yh * jnp.where(is_map0, r0, r1) * gain
            out_ref[:, hd * LANES:(hd + 1) * LANES] = _rope64(yn, cosd, sind).astype(BF16)

    diff_qk(C_DQ, dqg_ref, dq_ref)
    diff_qk(C_DK, dkg_ref, dk_ref)
    dv_ref[...] = _dot(h, win_ref[:, C_DV:C_DV + DIFF_HEADS * LANES]).astype(BF16)

    lat = _dot(h, win_ref[:, C_CQ:C_XQ])
    cq = lat[:, 0:MLA_Q_LORA]
    ckv = lat[:, MLA_Q_LORA:MLA_Q_LORA + MLA_KV_LORA]
    kr = lat[:, MLA_Q_LORA + MLA_KV_LORA:]
    cqn = (cq * _inv_rms(cq, MLA_Q_LORA) * qag_ref[...]).astype(BF16)
    ckvn = (ckv * _inv_rms(ckv, MLA_KV_LORA) * kvag_ref[...]).astype(BF16)
    kr = _rope64(kr, cosm, sinm)

    q = _dot(cqn, wqb_ref[...])
    kn = _dot(ckvn, wkvk_ref[...])
    mv_ref[...] = _dot(ckvn, wkvv_ref[...]).astype(BF16)
    mqg, mkg = mqg_ref[...], mkg_ref[...]
    for hd in range(MLA_HEADS):
        sl = slice(hd * LANES, (hd + 1) * LANES)
        qh = _rope64(q[:, sl], cosm, sinm)
        mq_ref[:, sl] = (qh * _inv_rms(qh, MLA_QK_DIM) * mqg).astype(BF16)
        kh = kn[:, sl] + kr
        mk_ref[:, sl] = (kh * _inv_rms(kh, MLA_QK_DIM) * mkg).astype(BF16)

    xq = _dot(h, win_ref[:, C_XQ:C_GATE])
    xqg = xqg_ref[...]
    for hd in range(CROSS_HEADS):
        sl = slice(hd * LANES, (hd + 1) * LANES)
        yh = xq[:, sl]
        xq_ref[:, sl] = (yh * _inv_rms(yh, CROSS_HEAD_DIM) * xqg).astype(BF16)

    for br in range(3):
        sl = slice(br * D_MODEL, (br + 1) * D_MODEL)
        z = _dot(h, win_ref[:, C_GATE + br * D_MODEL:C_GATE + (br + 1) * D_MODEL]) + bgate_ref[:, sl]
        gate_ref[:, sl] = (1.0 / (1.0 + jnp.exp(-z))).astype(BF16)


def _inproj(x, tabs, p):
    t = x.shape[0]
    tm = TM_IN
    row = lambda w: pl.BlockSpec((tm, w), lambda i: (i, 0))
    consts = [p["g_mix"], p["w_in"], p["b_gate"], p["dq_gain"], p["dk_gain"],
              p["qa_gain"], p["w_qb"], p["kva_gain"], p["w_kvk"], p["w_kvv"],
              p["mq_gain"], p["mk_gain"], p["xq_gain"]]
    out_widths = [DIFF_HEADS * LANES] * 3 + [MLA_HEADS * LANES] * 2 + [MLA_HEADS * MLA_V_DIM,
                                                                      CROSS_HEADS * CROSS_HEAD_DIM, 3 * D_MODEL]
    return pl.pallas_call(
        _inproj_kernel,
        grid=(t // tm,),
        in_specs=[row(D_MODEL)] + [row(LANES)] * 4 + [_const_spec(c.shape) for c in consts],
        out_specs=[row(w) for w in out_widths],
        out_shape=[jax.ShapeDtypeStruct((t, w), BF16) for w in out_widths],
        compiler_params=pltpu.CompilerParams(dimension_semantics=("parallel",),
                                             vmem_limit_bytes=VMEM_LIMIT),
        name="inproj",
    )(x, *tabs, *consts)


def _causal_bias_t(tq):
    r = lax.broadcasted_iota(jnp.int32, (tq, tq), 0)
    c = lax.broadcasted_iota(jnp.int32, (tq, tq), 1)
    return jnp.where(r <= c, 0.0, -jnp.inf).astype(F32)


def _fold_rows(a, op):
    n, w = a.shape
    return op(op(a.reshape(n // SUBLANES, SUBLANES, w), axis=0), axis=0, keepdims=True)


def _causal_attention(n_tiles, tq, k_ref, vt_ref, s_ref, p_ref, make_kq, finish):
    bias_t = _causal_bias_t(tq)
    w = s_ref.shape[2]

    def scores(i):
        lo, hi = i * tq, (i + 1) * tq
        s_buf = s_ref.at[i % 2]
        for k_sl, q_rows, s_sl in make_kq(i):
            reps = q_rows.shape[0] // tq
            bias = bias_t if reps == 1 else jnp.concatenate([bias_t] * reps, axis=1)
            s_buf[lo:hi, s_sl] = _dot_nt(k_ref[lo:hi, k_sl], q_rows) + bias
            if lo > 0:
                s_buf[0:lo, s_sl] = _dot_nt(k_ref[0:lo, k_sl], q_rows)

    def softmax(i):
        hi = (i + 1) * tq
        s_buf, p_buf = s_ref.at[i % 2], p_ref.at[i % 2]
        m = _fold_rows(s_buf[0:hi, :], jnp.max)
        l8 = jnp.zeros((SUBLANES, w), F32)
        for r in range(0, hi, SM_ROWS):
            p = jnp.exp2(s_buf[r:r + SM_ROWS, :] - m)
            l8 = l8 + jnp.sum(p.reshape(SM_ROWS // SUBLANES, SUBLANES, w), axis=0)
            p_buf[r:r + SM_ROWS, :] = p.astype(BF16)
        return jnp.sum(l8, axis=0, keepdims=True)

    def pv(i, l):
        hi = (i + 1) * tq
        finish(i, _dot(vt_ref[:, 0:hi], p_ref[i % 2, 0:hi, :]), l)

    scores(0)
    l_prev = None
    for i in range(n_tiles):
        if i + 1 < n_tiles:
            scores(i + 1)
        l_cur = softmax(i)
        if i > 0:
            pv(i - 1, l_prev)
        l_prev = l_cur
    pv(n_tiles - 1, l_prev)


def _store_vt(v_ref, vt_ref):
    vt_ref[...] = v_ref[...].astype(F32).T.astype(BF16)


def _diff_attn_kernel(lam_ref, og_ref, q_ref, k_ref, v_ref, o_ref, vt_ref, s_ref, p_ref):
    s_len = q_ref.shape[0]
    tq = TQ
    lane = lax.broadcasted_iota(jnp.int32, (tq, LANES), 1)
    is_map0 = (lane & 32) == 0
    _store_vt(v_ref, vt_ref)

    lv = lam_ref[...]
    lam = (jnp.exp(jnp.sum(lv[0:1] * lv[1:2], axis=-1, keepdims=True))
           - jnp.exp(jnp.sum(lv[2:3] * lv[3:4], axis=-1, keepdims=True))
           + lv[4:5, 0:1])
    og = og_ref[...] * (1.0 - lv[4:5, 0:1])
    og = jnp.concatenate([og] * (tq // LANES), axis=1)

    def make_kq(i):
        q = q_ref[i * tq:(i + 1) * tq, :]
        zero = jnp.zeros_like(q)
        qq = jnp.concatenate([jnp.where(is_map0, q, zero), jnp.where(is_map0, zero, q)], axis=0)
        return [(slice(None), qq, slice(None))]

    def finish(i, acc, l):
        o = acc / l
        o = o[:, 0:tq] - lam * o[:, tq:2 * tq]
        r = lax.rsqrt(jnp.sum(o * o, axis=0, keepdims=True) * (1.0 / (2 * DIFF_HEAD_DIM)) + EPS)
        o_ref[i * tq:(i + 1) * tq, :] = (o * r * og).T.astype(BF16)

    _causal_attention(s_len // tq, tq, k_ref, vt_ref, s_ref, p_ref, make_kq, finish)


def _attn_scratch(seq):
    return [pltpu.VMEM((LANES, seq), BF16),
            pltpu.VMEM((2, seq, 2 * TQ), F32),
            pltpu.VMEM((2, seq, 2 * TQ), BF16)]


def _diff_attn(lam_tab, out_gain_rows, dq, dk, dv, batch, seq):
    blk = pl.BlockSpec((None, seq, LANES), lambda b, h: (b, 0, h))
    shp = (batch, seq, DIFF_HEADS * LANES)
    out = pl.pallas_call(
        _diff_attn_kernel,
        grid=(batch, DIFF_HEADS),
        in_specs=[_const_spec(lam_tab.shape), _const_spec(out_gain_rows.shape), blk, blk, blk],
        out_specs=blk,
        out_shape=jax.ShapeDtypeStruct(shp, BF16),
        scratch_shapes=_attn_scratch(seq),
        compiler_params=pltpu.CompilerParams(dimension_semantics=("parallel", "parallel"),
                                             vmem_limit_bytes=VMEM_LIMIT),
        name="diff_attn",
    )(lam_tab, out_gain_rows, dq.reshape(shp), dk.reshape(shp), dv.reshape(shp))
    return out.reshape(batch * seq, DIFF_HEADS * LANES)


def _mla_attn_kernel(q_ref, k_ref, v_ref, o_ref, vt_ref, s_ref, p_ref):
    s_len = q_ref.shape[0]
    tq = TQ
    _store_vt(v_ref, vt_ref)
    row = lax.broadcasted_iota(jnp.int32, (LANES, tq), 0)
    is_head0 = row < MLA_V_DIM

    def make_kq(i):
        return [(slice(hd * LANES, (hd + 1) * LANES), q_ref[i * tq:(i + 1) * tq, hd * LANES:(hd + 1) * LANES],
                 slice(hd * tq, (hd + 1) * tq)) for hd in range(2)]

    def finish(i, acc, l):
        o = acc / l
        o = jnp.where(is_head0, o[:, 0:tq], o[:, tq:2 * tq])
        o_ref[i * tq:(i + 1) * tq, :] = o.T.astype(BF16)

    _causal_attention(s_len // tq, tq, k_ref, vt_ref, s_ref, p_ref, make_kq, finish)


def _mla_attn(mq, mk, mv, batch, seq):
    qk_blk = pl.BlockSpec((None, seq, 2 * LANES), lambda b, h: (b, 0, h))
    v_blk = pl.BlockSpec((None, seq, 2 * MLA_V_DIM), lambda b, h: (b, 0, h))
    qk_shp = (batch, seq, MLA_HEADS * LANES)
    v_shp = (batch, seq, MLA_HEADS * MLA_V_DIM)
    out = pl.pallas_call(
        _mla_attn_kernel,
        grid=(batch, MLA_HEADS // 2),
        in_specs=[qk_blk, qk_blk, v_blk],
        out_specs=v_blk,
        out_shape=jax.ShapeDtypeStruct(v_shp, BF16),
        scratch_shapes=_attn_scratch(seq),
        compiler_params=pltpu.CompilerParams(dimension_semantics=("parallel", "parallel"),
                                             vmem_limit_bytes=VMEM_LIMIT),
        name="mla_attn",
    )(mq.reshape(qk_shp), mk.reshape(qk_shp), mv.reshape(v_shp))
    return out.reshape(batch * seq, MLA_HEADS * MLA_V_DIM)


def _mem_kv_kernel(mem_ref, mg_ref, w_ref, kg_ref, k_ref, v_ref):
    m = mem_ref[...]
    mn = (m * _inv_rms(m, D_MODEL) * mg_ref[...]).astype(BF16)
    kv = _dot(mn, w_ref[...])
    width = CROSS_HEADS * CROSS_HEAD_DIM
    kg = kg_ref[...]
    for hd in range(CROSS_HEADS):
        sl = slice(hd * LANES, (hd + 1) * LANES)
        kh = kv[:, sl]
        k_ref[:, sl] = (kh * _inv_rms(kh, CROSS_HEAD_DIM) * kg).astype(BF16)
    v_ref[...] = kv[:, width:].astype(BF16)


def _mem_kv(mem2d, p):
    rows = mem2d.shape[0]
    width = CROSS_HEADS * CROSS_HEAD_DIM
    tm = min(512, rows)
    return pl.pallas_call(
        _mem_kv_kernel,
        grid=(rows // tm,),
        in_specs=[pl.BlockSpec((tm, D_MODEL), lambda i: (i, 0)), _const_spec(p["mem_gain"].shape),
                  _const_spec(p["w_mem_kv"].shape), _const_spec(p["ck_gain"].shape)],
        out_specs=[pl.BlockSpec((tm, width), lambda i: (i, 0))] * 2,
        out_shape=[jax.ShapeDtypeStruct((rows, width), BF16)] * 2,
        compiler_params=pltpu.CompilerParams(dimension_semantics=("parallel",),
                                             vmem_limit_bytes=VMEM_LIMIT),
        name="mem_kv",
    )(mem2d, p["mem_gain"], p["w_mem_kv"], p["ck_gain"])


def _cross_attn_kernel(q_ref, k_ref, v_ref, o_ref):
    for hd in range(CROSS_HEADS):
        sl = slice(hd * LANES, (hd + 1) * LANES)
        s = _dot_nt(q_ref[:, sl], k_ref[:, sl])
        p = jnp.exp2(s - jnp.max(s, axis=-1, keepdims=True))
        l = jnp.sum(p, axis=-1, keepdims=True)
        o_ref[:, sl] = (_dot(p.astype(BF16), v_ref[:, sl]) / l).astype(BF16)


def _cross_attn(xq, ck, cv, batch, seq):
    width = CROSS_HEADS * CROSS_HEAD_DIM
    tq = min(TQ_CROSS, seq)
    q_blk = pl.BlockSpec((None, tq, width), lambda b, i: (b, i, 0))
    kv_blk = pl.BlockSpec((None, MEM_LEN, width), lambda b, i: (b, 0, 0))
    out = pl.pallas_call(
        _cross_attn_kernel,
        grid=(batch, seq // tq),
        in_specs=[q_blk, kv_blk, kv_blk],
        out_specs=q_blk,
        out_shape=jax.ShapeDtypeStruct((batch, seq, width), BF16),
        compiler_params=pltpu.CompilerParams(dimension_semantics=("parallel", "parallel"),
                                             vmem_limit_bytes=VMEM_LIMIT),
        name="cross_attn",
    )(xq.reshape(batch, seq, width), ck.reshape(batch, MEM_LEN, width), cv.reshape(batch, MEM_LEN, width))
    return out.reshape(batch * seq, width)


def _outproj_kernel(x_ref, od_ref, om_ref, oc_ref, g_ref, wdo_ref, wmo_ref, wco_ref, wout_ref, o_ref):
    merged = (g_ref[:, 0:D_MODEL].astype(F32) * _dot(od_ref[...], wdo_ref[...])
              + g_ref[:, D_MODEL:2 * D_MODEL].astype(F32) * _dot(om_ref[...], wmo_ref[...])
              + g_ref[:, 2 * D_MODEL:].astype(F32) * _dot(oc_ref[...], wco_ref[...]))
    o_ref[...] = x_ref[...] + _dot(merged.astype(BF16), wout_ref[...])


def _outproj(x, od, om, oc, gates, p):
    t = x.shape[0]
    tm = min(TM_OUT, t)
    row = lambda w: pl.BlockSpec((tm, w), lambda i: (i, 0))
    consts = [p["w_diff_o"], p["w_mla_o"], p["w_cross_o"], p["w_out"]]
    return pl.pallas_call(
        _outproj_kernel,
        grid=(t // tm,),
        in_specs=[row(D_MODEL), row(od.shape[1]), row(om.shape[1]), row(oc.shape[1]), row(gates.shape[1])]
                 + [_const_spec(c.shape) for c in consts],
        out_specs=row(D_MODEL),
        out_shape=jax.ShapeDtypeStruct((t, D_MODEL), F32),
        compiler_params=pltpu.CompilerParams(dimension_semantics=("parallel",),
                                             vmem_limit_bytes=VMEM_LIMIT),
        name="outproj",
    )(x, od, om, oc, gates, *consts)


FF_CHUNK = 1024


def _mlp_kernel(x_ref, g_ref, w1_ref, w2_ref, o_ref):
    x = x_ref[...]
    h = (x * _inv_rms(x, D_MODEL) * g_ref[...]).astype(BF16)
    y = x
    for c in range(D_FF // FF_CHUNK):
        sl = slice(c * FF_CHUNK, (c + 1) * FF_CHUNK)
        u = jnp.maximum(_dot(h, w1_ref[:, sl]), 0.0)
        y = y + _dot((u * u).astype(BF16), w2_ref[sl, :])
    o_ref[...] = y


def _mlp(x, p):
    t = x.shape[0]
    tm = min(TM_OUT, t)
    row = pl.BlockSpec((tm, D_MODEL), lambda i: (i, 0))
    consts = [p["g_mlp"], p["w_mlp1"], p["w_mlp2"]]
    return pl.pallas_call(
        _mlp_kernel,
        grid=(t // tm,),
        in_specs=[row] + [_const_spec(c.shape) for c in consts],
        out_specs=row,
        out_shape=jax.ShapeDtypeStruct((t, D_MODEL), F32),
        compiler_params=pltpu.CompilerParams(dimension_semantics=("parallel",),
                                             vmem_limit_bytes=VMEM_LIMIT),
        name="mlp",
    )(x, *consts)


def _pack_params(w_in, b_gate, g_mix, diff_q_gain, diff_k_gain, diff_lambda, diff_out_gain, w_diff_o,
                 mla_qa_gain, w_mla_qb, mla_kva_gain, w_mla_kvb, mla_q_gain, mla_k_gain, w_mla_o,
                 mem_gain, w_mem_kv, cross_q_gain, cross_k_gain, w_cross_o, w_out, g_mlp, w_mlp1, w_mlp2):
    nl = w_in.shape[0]
    d = D_MODEL
    half = DIFF_HEAD_DIM // 2
    o_dq, o_dk, o_dv = 0, 1024, 2048
    o_cq = 3072
    o_ckv = o_cq + MLA_Q_LORA
    o_kr = o_ckv + MLA_KV_LORA
    o_xq = o_kr + MLA_ROPE_DIM

    def diff_cols(w):
        r = w.shape[1]
        return w.reshape(nl, r, DIFF_HEADS, 2, 2, half).transpose(0, 1, 2, 4, 3, 5).reshape(
            nl, r, DIFF_HEADS * LANES)

    def diff_gain(g, scale):
        g = jnp.broadcast_to(g.reshape(nl, 2, 1, half), (nl, 2, 2, half)).reshape(nl, 1, LANES)
        return (g * scale).astype(F32)

    rh = MLA_ROPE_DIM // 2

    def mla_head_layout(nope, rope):
        pad = jnp.zeros(nope.shape[:-1] + (LANES - MLA_QK_DIM,), nope.dtype)
        return jnp.concatenate([rope[..., :rh], nope[..., :HALF_LANES - rh], rope[..., rh:],
                                nope[..., HALF_LANES - rh:], pad], axis=-1)

    kr_w = w_in[:, :, o_kr:o_xq]
    zpad = jnp.zeros((nl, d, HALF_LANES - rh), F32)
    kr_cols = jnp.concatenate([kr_w[..., :rh], zpad, kr_w[..., rh:], zpad], axis=-1)
    w_in_p = jnp.concatenate([diff_cols(w_in[:, :, o_dq:o_dk]).astype(BF16),
                              diff_cols(w_in[:, :, o_dk:o_dv]).astype(BF16),
                              w_in[:, :, o_dv:o_kr].astype(BF16), kr_cols.astype(BF16),
                              w_in[:, :, o_xq:].astype(BF16)], axis=-1)

    wqb = w_mla_qb.reshape(nl, MLA_Q_LORA, MLA_HEADS, MLA_QK_DIM)
    w_qb_p = mla_head_layout(wqb[..., :MLA_NOPE_DIM], wqb[..., MLA_NOPE_DIM:]).reshape(
        nl, MLA_Q_LORA, MLA_HEADS * LANES).astype(BF16)
    wkvb = w_mla_kvb.reshape(nl, MLA_KV_LORA, MLA_HEADS, MLA_NOPE_DIM + MLA_V_DIM)
    k_nope = wkvb[..., :MLA_NOPE_DIM]
    w_kvk_p = mla_head_layout(k_nope, jnp.zeros(k_nope.shape[:-1] + (MLA_ROPE_DIM,), F32)).reshape(
        nl, MLA_KV_LORA, MLA_HEADS * LANES).astype(BF16)
    w_kvv_p = wkvb[..., MLA_NOPE_DIM:].reshape(nl, MLA_KV_LORA, MLA_HEADS * MLA_V_DIM).astype(BF16)

    def mla_gain(g, scale):
        return (mla_head_layout(g[:, :MLA_NOPE_DIM], g[:, MLA_NOPE_DIM:]) * scale)[:, None, :].astype(F32)

    lam_init = jnp.asarray([0.8 - 0.6 * math.exp(-0.3 * l) for l in range(nl)], F32)
    lam_tab = jnp.zeros((nl, SUBLANES, LANES), F32)
    lam_tab = lam_tab.at[:, 0:4, 0:DIFF_HEAD_DIM].set(diff_lambda.astype(F32))
    lam_tab = lam_tab.at[:, 4, :].set(lam_init[:, None])
    out_gain_rows = jnp.broadcast_to(diff_out_gain[:, :, None], (nl, 2 * DIFF_HEAD_DIM, LANES)).astype(F32)

    row = lambda g: g[:, None, :]
    return dict(
        g_mix=row(g_mix), w_in=w_in_p, b_gate=row(b_gate),
        dq_gain=diff_gain(diff_q_gain, DIFF_HEAD_DIM ** -0.5 * LOG2E), dk_gain=diff_gain(diff_k_gain, 1.0),
        qa_gain=row(mla_qa_gain), w_qb=w_qb_p, kva_gain=row(mla_kva_gain),
        w_kvk=w_kvk_p, w_kvv=w_kvv_p,
        mq_gain=mla_gain(mla_q_gain, MLA_QK_DIM ** -0.5 * LOG2E), mk_gain=mla_gain(mla_k_gain, 1.0),
        xq_gain=row(cross_q_gain * (CROSS_HEAD_DIM ** -0.5 * LOG2E)),
        lam_tab=lam_tab, out_gain_rows=out_gain_rows,
        w_diff_o=w_diff_o.astype(BF16), w_mla_o=w_mla_o.astype(BF16),
        mem_gain=row(mem_gain), w_mem_kv=w_mem_kv.astype(BF16), ck_gain=row(cross_k_gain),
        w_cross_o=w_cross_o.astype(BF16), w_out=w_out.astype(BF16),
        g_mlp=row(g_mlp), w_mlp1=w_mlp1.astype(BF16), w_mlp2=w_mlp2.astype(BF16),
    )


def _rope_lane_tables(positions):
    cos, sin = _rope_tables(positions)
    t = cos.shape[0]
    cosd = jnp.tile(cos, (1, 4))
    sind = jnp.concatenate([-sin, -sin, sin, sin], axis=1)
    rh = MLA_ROPE_DIM // 2
    cos_r, sin_r = cos[:, ::2], sin[:, ::2]
    ones = jnp.ones((t, HALF_LANES - rh), F32)
    zeros = jnp.zeros((t, HALF_LANES - rh), F32)
    cosm = jnp.concatenate([cos_r, ones, cos_r, ones], axis=1)
    sinm = jnp.concatenate([-sin_r, zeros, sin_r, zeros], axis=1)
    return cosd, sind, cosm, sinm


def kernel(x, mem, positions, g_mix, w_in, b_gate, diff_q_gain, diff_k_gain, diff_lambda, diff_out_gain,
           w_diff_o, mla_qa_gain, w_mla_qb, mla_kva_gain, w_mla_kvb, mla_q_gain, mla_k_gain, w_mla_o,
           mem_gain, w_mem_kv, cross_q_gain, cross_k_gain, w_cross_o, w_out, g_mlp, w_mlp1, w_mlp2):
    batch, seq, d = x.shape
    tabs = _rope_lane_tables(positions)
    packed = _pack_params(w_in, b_gate, g_mix, diff_q_gain, diff_k_gain, diff_lambda, diff_out_gain,
                          w_diff_o, mla_qa_gain, w_mla_qb, mla_kva_gain, w_mla_kvb, mla_q_gain, mla_k_gain,
                          w_mla_o, mem_gain, w_mem_kv, cross_q_gain, cross_k_gain, w_cross_o, w_out,
                          g_mlp, w_mlp1, w_mlp2)
    xt = x.reshape(batch * seq, d)
    mem2d = mem.reshape(batch * mem.shape[1], d)
    for l in range(w_in.shape[0]):
        p = {name: a[l] for name, a in packed.items()}
        dq, dk, dv, mq, mk, mv, xq, gates = _inproj(xt, tabs, p)
        od = _diff_attn(p["lam_tab"], p["out_gain_rows"], dq, dk, dv, batch, seq)
        om = _mla_attn(mq, mk, mv, batch, seq)
        ck, cv = _mem_kv(mem2d, p)
        oc = _cross_attn(xq, ck, cv, batch, seq)
        xt = _outproj(xt, od, om, oc, gates, p)
        xt = _mlp(xt, p)
    return xt.reshape(batch, seq, d)
```

```python
import math

import jax
import jax.numpy as jnp
from jax import lax
from jax.experimental import pallas as pl
from jax.experimental.pallas import tpu as pltpu

D_MODEL = 1024
DEPTH = 4
MEM_LEN = 256
ROPE_THETA = 10000.0
EPS = 1e-6

DIFF_HEADS = 8
DIFF_HEAD_DIM = 64
MLA_HEADS = 8
MLA_Q_LORA = 384
MLA_KV_LORA = 256
MLA_NOPE_DIM = 64
MLA_ROPE_DIM = 32
MLA_V_DIM = 64
MLA_QK_DIM = MLA_NOPE_DIM + MLA_ROPE_DIM
CROSS_HEADS = 4
CROSS_HEAD_DIM = 128
D_FF = 4 * D_MODEL

LANES = 128
SUBLANES = 8
HALF_LANES = LANES // 2
LOG2E = math.log2(math.e)

C_DQ = 0
C_DK = C_DQ + DIFF_HEADS * LANES
C_DV = C_DK + DIFF_HEADS * LANES
C_CQ = C_DV + DIFF_HEADS * LANES
C_CKV = C_CQ + MLA_Q_LORA
C_KR = C_CKV + MLA_KV_LORA
C_XQ = C_KR + LANES
C_GATE = C_XQ + CROSS_HEADS * CROSS_HEAD_DIM
IN_WIDTH_PACKED = C_GATE + 3 * D_MODEL

TM_IN = 512
TM_IN_SUB = 512
TM_OUT = 512
TQ = 256
SM_ROWS = 16
VMEM_LIMIT = 56 * 1024 * 1024

BF16 = jnp.bfloat16
F32 = jnp.float32


def _const_spec(shape):
    nd = len(shape)
    return pl.BlockSpec(shape, lambda *_: (0,) * nd, pipeline_mode=pl.Buffered(1))


def _dot(a, b):
    return jnp.dot(a, b, preferred_element_type=F32)


def _dot_nt(a, b):
    return lax.dot_general(a, b, (((1,), (1,)), ((), ())), preferred_element_type=F32)


def _inv_rms(y, n):
    return lax.rsqrt(jnp.sum(y * y, axis=-1, keepdims=True) * (1.0 / n) + EPS)


def _rope64(y, cos, sin):
    return y * cos + pltpu.roll(y, HALF_LANES, axis=1) * sin


def _rope_table_kernel(pos_ref, invf_ref, cos_ref, sin_ref):
    ang = pos_ref[...].astype(F32) * invf_ref[...]
    cos_ref[...] = jnp.cos(ang)
    sin_ref[...] = jnp.sin(ang)


def _rope_tables(positions):
    nfreq = DIFF_HEAD_DIM // 2
    per_row = LANES // nfreq
    t = positions.size
    inv_freq = ROPE_THETA ** (-jnp.arange(0, DIFF_HEAD_DIM, 2, dtype=F32) / DIFF_HEAD_DIM)
    inv_freq = jnp.concatenate([inv_freq[0::2], inv_freq[1::2]])
    pos_rep = jnp.repeat(positions.reshape(t // per_row, per_row), nfreq, axis=1)
    invf = jnp.tile(inv_freq, per_row)[None, :]
    rows = t // per_row
    tr = min(2048, rows)
    cos, sin = pl.pallas_call(
        _rope_table_kernel,
        grid=(rows // tr,),
        in_specs=[pl.BlockSpec((tr, LANES), lambda i: (i, 0)),
                  pl.BlockSpec((1, LANES), lambda i: (0, 0))],
        out_specs=[pl.BlockSpec((tr, LANES), lambda i: (i, 0))] * 2,
        out_shape=[jax.ShapeDtypeStruct((rows, LANES), F32)] * 2,
        name="rope_tables",
    )(pos_rep, invf)
    return cos.reshape(t, nfreq), sin.reshape(t, nfreq)


def _inproj_kernel(x_ref, cosd_ref, sind_ref, cosm_ref, sinm_ref,
                   gmix_ref, win_ref, bgate_ref, dqg_ref, dkg_ref,
                   qag_ref, wqb_ref, kvag_ref, wkvk_ref, wkvv_ref, mqg_ref, mkg_ref, xqg_ref,
                   dq_ref, dk_ref, dv_ref, mq_ref, mk_ref, mv_ref, xq_ref, gate_ref):
    for r0 in range(0, x_ref.shape[0], TM_IN_SUB):
        _inproj_rows(slice(r0, r0 + TM_IN_SUB), x_ref, cosd_ref, sind_ref, cosm_ref, sinm_ref,
                     gmix_ref, win_ref, bgate_ref, dqg_ref, dkg_ref,
                     qag_ref, wqb_ref, kvag_ref, wkvk_ref, wkvv_ref, mqg_ref, mkg_ref, xqg_ref,
                     dq_ref, dk_ref, dv_ref, mq_ref, mk_ref, mv_ref, xq_ref, gate_ref)


def _inproj_rows(rs, x_ref, cosd_ref, sind_ref, cosm_ref, sinm_ref,
                 gmix_ref, win_ref, bgate_ref, dqg_ref, dkg_ref,
                 qag_ref, wqb_ref, kvag_ref, wkvk_ref, wkvv_ref, mqg_ref, mkg_ref, xqg_ref,
                 dq_ref, dk_ref, dv_ref, mq_ref, mk_ref, mv_ref, xq_ref, gate_ref):
    x = x_ref[rs, :]
    h = (x * _inv_rms(x, D_MODEL) * gmix_ref[...]).astype(BF16)
    tm = x.shape[0]

    cosd, sind = cosd_ref[rs, :], sind_ref[rs, :]
    cosm, sinm = cosm_ref[rs, :], sinm_ref[rs, :]
    lane = lax.broadcasted_iota(jnp.int32, (tm, LANES), 1)
    is_map0 = (lane & 32) == 0

    def diff_qk(col0, gain_ref, out_ref):
        y = _dot(h, win_ref[:, col0:col0 + DIFF_HEADS * LANES])
        gain = gain_ref[...]
        for hd in range(DIFF_HEADS):
            yh = y[:, hd * LANES:(hd + 1) * LANES]
            sq = yh * yh
            s_all = jnp.sum(sq, axis=-1, keepdims=True)
            s0 = jnp.sum(jnp.where(is_map0, sq, 0.0), axis=-1, keepdims=True)
            r0 = lax.rsqrt(s0 * (1.0 / DIFF_HEAD_DIM) + EPS)
            r1 = lax.rsqrt((s_all - s0) * (1.0 / DIFF_HEAD_DIM) + EPS)
            yn = yh * jnp.where(is_map0, r0, r1) * gain
            out_ref[rs, hd * LANES:(hd + 1) * LANES] = _rope64(yn, cosd, sind).astype(BF16)

    diff_qk(C_DQ, dqg_ref, dq_ref)
    diff_qk(C_DK, dkg_ref, dk_ref)
    dv_ref[rs, :] = _dot(h, win_ref[:, C_DV:C_DV + DIFF_HEADS * LANES]).astype(BF16)

    lat = _dot(h, win_ref[:, C_CQ:C_XQ])
    cq = lat[:, 0:MLA_Q_LORA]
    ckv = lat[:, MLA_Q_LORA:MLA_Q_LORA + MLA_KV_LORA]
    kr = lat[:, MLA_Q_LORA + MLA_KV_LORA:]
    cqn = (cq * _inv_rms(cq, MLA_Q_LORA) * qag_ref[...]).astype(BF16)
    ckvn = (ckv * _inv_rms(ckv, MLA_KV_LORA) * kvag_ref[...]).astype(BF16)
    kr = _rope64(kr, cosm, sinm)

    q = _dot(cqn, wqb_ref[...])
    kn = _dot(ckvn, wkvk_ref[...])
    mv_ref[rs, :] = _dot(ckvn, wkvv_ref[...]).astype(BF16)
    mqg, mkg = mqg_ref[...], mkg_ref[...]
    for hd in range(MLA_HEADS):
        sl = slice(hd * LANES, (hd + 1) * LANES)
        qh = _rope64(q[:, sl], cosm, sinm)
        mq_ref[rs, sl] = (qh * _inv_rms(qh, MLA_QK_DIM) * mqg).astype(BF16)
        kh = kn[:, sl] + kr
        mk_ref[rs, sl] = (kh * _inv_rms(kh, MLA_QK_DIM) * mkg).astype(BF16)

    xq = _dot(h, win_ref[:, C_XQ:C_GATE])
    xqg = xqg_ref[...]
    for hd in range(CROSS_HEADS):
        sl = slice(hd * LANES, (hd + 1) * LANES)
        yh = xq[:, sl]
        xq_ref[rs, sl] = (yh * _inv_rms(yh, CROSS_HEAD_DIM) * xqg).astype(BF16)

    for br in range(3):
        sl = slice(br * D_MODEL, (br + 1) * D_MODEL)
        z = _dot(h, win_ref[:, C_GATE + br * D_MODEL:C_GATE + (br + 1) * D_MODEL]) + bgate_ref[:, sl]
        gate_ref[rs, sl] = (1.0 / (1.0 + jnp.exp(-z))).astype(BF16)


def _inproj(x, tabs, p):
    t = x.shape[0]
    tm = min(TM_IN, t)
    row = lambda w: pl.BlockSpec((tm, w), lambda i: (i, 0))
    consts = [p["g_mix"], p["w_in"], p["b_gate"], p["dq_gain"], p["dk_gain"],
              p["qa_gain"], p["w_qb"], p["kva_gain"], p["w_kvk"], p["w_kvv"],
              p["mq_gain"], p["mk_gain"], p["xq_gain"]]
    out_widths = [DIFF_HEADS * LANES] * 3 + [MLA_HEADS * LANES] * 2 + [MLA_HEADS * MLA_V_DIM,
                                                                      CROSS_HEADS * CROSS_HEAD_DIM, 3 * D_MODEL]
    return pl.pallas_call(
        _inproj_kernel,
        grid=(t // tm,),
        in_specs=[row(D_MODEL)] + [row(LANES)] * 4 + [_const_spec(c.shape) for c in consts],
        out_specs=[row(w) for w in out_widths],
        out_shape=[jax.ShapeDtypeStruct((t, w), BF16) for w in out_widths],
        compiler_params=pltpu.CompilerParams(dimension_semantics=("parallel",),
                                             vmem_limit_bytes=VMEM_LIMIT),
        name="inproj",
    )(x, *tabs, *consts)


def _causal_bias_t(tq):
    r = lax.broadcasted_iota(jnp.int32, (tq, tq), 0)
    c = lax.broadcasted_iota(jnp.int32, (tq, tq), 1)
    return jnp.where(r <= c, 0.0, -jnp.inf).astype(F32)


def _fold_rows(a, op):
    n, w = a.shape
    return op(op(a.reshape(n // SUBLANES, SUBLANES, w), axis=0), axis=0, keepdims=True)


def _causal_attention(n_tiles, tq, k_ref, vt_ref, s_ref, p_ref, make_kq, finish):
    bias_t = _causal_bias_t(tq)
    w = s_ref.shape[2]

    def scores(i):
        lo, hi = i * tq, (i + 1) * tq
        s_buf = s_ref.at[i % 2]
        for k_sl, q_rows, s_sl in make_kq(i):
            reps = q_rows.shape[0] // tq
            bias = bias_t if reps == 1 else jnp.concatenate([bias_t] * reps, axis=1)
            s_buf[lo:hi, s_sl] = _dot_nt(k_ref[lo:hi, k_sl], q_rows) + bias
            if lo > 0:
                s_buf[0:lo, s_sl] = _dot_nt(k_ref[0:lo, k_sl], q_rows)

    def softmax(i):
        hi = (i + 1) * tq
        s_buf, p_buf = s_ref.at[i % 2], p_ref.at[i % 2]
        m = _fold_rows(s_buf[0:hi, :], jnp.max)
        l8 = jnp.zeros((SUBLANES, w), F32)
        for r in range(0, hi, SM_ROWS):
            p = jnp.exp2(s_buf[r:r + SM_ROWS, :] - m)
            l8 = l8 + jnp.sum(p.reshape(SM_ROWS // SUBLANES, SUBLANES, w), axis=0)
            p_buf[r:r + SM_ROWS, :] = p.astype(BF16)
        return jnp.sum(l8, axis=0, keepdims=True)

    def pv(i, l):
        hi = (i + 1) * tq
        finish(i, _dot(vt_ref[:, 0:hi], p_ref[i % 2, 0:hi, :]), l)

    scores(0)
    l_prev = None
    for i in range(n_tiles):
        if i + 1 < n_tiles:
            scores(i + 1)
        l_cur = softmax(i)
        if i > 0:
            pv(i - 1, l_prev)
        l_prev = l_cur
    pv(n_tiles - 1, l_prev)


def _store_vt(v_ref, vt_ref):
    vt_ref[...] = v_ref[...].astype(F32).T.astype(BF16)


def _diff_attn_kernel(lam_ref, og_ref, q_ref, k_ref, v_ref, o_ref, vt_ref, s_ref, p_ref):
    s_len = q_ref.shape[0]
    tq = TQ
    lane = lax.broadcasted_iota(jnp.int32, (tq, LANES), 1)
    is_map0 = (lane & 32) == 0
    _store_vt(v_ref, vt_ref)

    lv = lam_ref[...]
    lam = (jnp.exp(jnp.sum(lv[0:1] * lv[1:2], axis=-1, keepdims=True))
           - jnp.exp(jnp.sum(lv[2:3] * lv[3:4], axis=-1, keepdims=True))
           + lv[4:5, 0:1])
    og = og_ref[...] * (1.0 - lv[4:5, 0:1])
    og = jnp.concatenate([og] * (tq // LANES), axis=1)

    def make_kq(i):
        q = q_ref[i * tq:(i + 1) * tq, :]
        zero = jnp.zeros_like(q)
        qq = jnp.concatenate([jnp.where(is_map0, q, zero), jnp.where(is_map0, zero, q)], axis=0)
        return [(slice(None), qq, slice(None))]

    def finish(i, acc, l):
        o = acc / l
        o = o[:, 0:tq] - lam * o[:, tq:2 * tq]
        r = lax.rsqrt(jnp.sum(o * o, axis=0, keepdims=True) * (1.0 / (2 * DIFF_HEAD_DIM)) + EPS)
        o_ref[i * tq:(i + 1) * tq, :] = (o * r * og).T.astype(BF16)

    _causal_attention(s_len // tq, tq, k_ref, vt_ref, s_ref, p_ref, make_kq, finish)


def _attn_scratch(seq):
    return [pltpu.VMEM((LANES, seq), BF16),
            pltpu.VMEM((2, seq, 2 * TQ), F32),
            pltpu.VMEM((2, seq, 2 * TQ), BF16)]


def _diff_attn(lam_tab, out_gain_rows, dq, dk, dv, batch, seq):
    blk = pl.BlockSpec((None, seq, LANES), lambda b, h: (b, 0, h))
    shp = (batch, seq, DIFF_HEADS * LANES)
    out = pl.pallas_call(
        _diff_attn_kernel,
        grid=(batch, DIFF_HEADS),
        in_specs=[_const_spec(lam_tab.shape), _const_spec(out_gain_rows.shape), blk, blk, blk],
        out_specs=blk,
        out_shape=jax.ShapeDtypeStruct(shp, BF16),
        scratch_shapes=_attn_scratch(seq),
        compiler_params=pltpu.CompilerParams(dimension_semantics=("parallel", "parallel"),
                                             vmem_limit_bytes=VMEM_LIMIT),
        name="diff_attn",
    )(lam_tab, out_gain_rows, dq.reshape(shp), dk.reshape(shp), dv.reshape(shp))
    return out.reshape(batch * seq, DIFF_HEADS * LANES)


def _mla_attn_kernel(q_ref, k_ref, v_ref, o_ref, vt_ref, s_ref, p_ref):
    s_len = q_ref.shape[0]
    tq = TQ
    _store_vt(v_ref, vt_ref)
    row = lax.broadcasted_iota(jnp.int32, (LANES, tq), 0)
    is_head0 = row < MLA_V_DIM

    def make_kq(i):
        return [(slice(hd * LANES, (hd + 1) * LANES), q_ref[i * tq:(i + 1) * tq, hd * LANES:(hd + 1) * LANES],
                 slice(hd * tq, (hd + 1) * tq)) for hd in range(2)]

    def finish(i, acc, l):
        o = acc / l
        o = jnp.where(is_head0, o[:, 0:tq], o[:, tq:2 * tq])
        o_ref[i * tq:(i + 1) * tq, :] = o.T.astype(BF16)

    _causal_attention(s_len // tq, tq, k_ref, vt_ref, s_ref, p_ref, make_kq, finish)


def _mla_attn(mq, mk, mv, batch, seq):
    qk_blk = pl.BlockSpec((None, seq, 2 * LANES), lambda b, h: (b, 0, h))
    v_blk = pl.BlockSpec((None, seq, 2 * MLA_V_DIM), lambda b, h: (b, 0, h))
    qk_shp = (batch, seq, MLA_HEADS * LANES)
    v_shp = (batch, seq, MLA_HEADS * MLA_V_DIM)
    out = pl.pallas_call(
        _mla_attn_kernel,
        grid=(batch, MLA_HEADS // 2),
        in_specs=[qk_blk, qk_blk, v_blk],
        out_specs=v_blk,
        out_shape=jax.ShapeDtypeStruct(v_shp, BF16),
        scratch_shapes=_attn_scratch(seq),
        compiler_params=pltpu.CompilerParams(dimension_semantics=("parallel", "parallel"),
                                             vmem_limit_bytes=VMEM_LIMIT),
        name="mla_attn",
    )(mq.reshape(qk_shp), mk.reshape(qk_shp), mv.reshape(v_shp))
    return out.reshape(batch * seq, MLA_HEADS * MLA_V_DIM)


def _mem_kv_kernel(mem_ref, mg_ref, w_ref, kg_ref, k_ref, v_ref):
    m = mem_ref[...]
    mn = (m * _inv_rms(m, D_MODEL) * mg_ref[...]).astype(BF16)
    kv = _dot(mn, w_ref[...])
    width = CROSS_HEADS * CROSS_HEAD_DIM
    kg = kg_ref[...]
    for hd in range(CROSS_HEADS):
        sl = slice(hd * LANES, (hd + 1) * LANES)
        kh = kv[:, sl]
        k_ref[:, sl] = (kh * _inv_rms(kh, CROSS_HEAD_DIM) * kg).astype(BF16)
    v_ref[...] = kv[:, width:].astype(BF16)


def _mem_kv(mem2d, p):
    rows = mem2d.shape[0]
    width = CROSS_HEADS * CROSS_HEAD_DIM
    tm = min(512, rows)
    return pl.pallas_call(
        _mem_kv_kernel,
        grid=(rows // tm,),
        in_specs=[pl.BlockSpec((tm, D_MODEL), lambda i: (i, 0)), _const_spec(p["mem_gain"].shape),
                  _const_spec(p["w_mem_kv"].shape), _const_spec(p["ck_gain"].shape)],
        out_specs=[pl.BlockSpec((tm, width), lambda i: (i, 0))] * 2,
        out_shape=[jax.ShapeDtypeStruct((rows, width), BF16)] * 2,
        compiler_params=pltpu.CompilerParams(dimension_semantics=("parallel",),
                                             vmem_limit_bytes=VMEM_LIMIT),
        name="mem_kv",
    )(mem2d, p["mem_gain"], p["w_mem_kv"], p["ck_gain"])


FF_CHUNK = 1024


def _post_kernel(x_ref, od_ref, om_ref, xq_ref, ck_ref, cv_ref, g_ref,
                 wdo_ref, wmo_ref, wco_ref, wout_ref, gmlp_ref, w1_ref, w2_ref, o_ref):
    heads = []
    for hd in range(CROSS_HEADS):
        sl = slice(hd * LANES, (hd + 1) * LANES)
        s = _dot_nt(xq_ref[:, sl], ck_ref[:, sl])
        p = jnp.exp2(s - jnp.max(s, axis=-1, keepdims=True))
        l = jnp.sum(p, axis=-1, keepdims=True)
        heads.append((_dot(p.astype(BF16), cv_ref[:, sl]) / l).astype(BF16))
    oc = jnp.concatenate(heads, axis=1)

    merged = (g_ref[:, 0:D_MODEL].astype(F32) * _dot(od_ref[...], wdo_ref[...])
              + g_ref[:, D_MODEL:2 * D_MODEL].astype(F32) * _dot(om_ref[...], wmo_ref[...])
              + g_ref[:, 2 * D_MODEL:].astype(F32) * _dot(oc, wco_ref[...]))
    x = x_ref[...] + _dot(merged.astype(BF16), wout_ref[...])

    h = (x * _inv_rms(x, D_MODEL) * gmlp_ref[...]).astype(BF16)
    y = x
    for c in range(D_FF // FF_CHUNK):
        sl = slice(c * FF_CHUNK, (c + 1) * FF_CHUNK)
        u = jnp.maximum(_dot(h, w1_ref[:, sl]), 0.0)
        y = y + _dot((u * u).astype(BF16), w2_ref[sl, :])
    o_ref[...] = y


def _post(x, od, om, xq, ck, cv, gates, p, seq):
    t = x.shape[0]
    tm = min(TM_OUT, seq)
    tiles_per_batch = seq // tm
    width = CROSS_HEADS * CROSS_HEAD_DIM
    row = lambda w: pl.BlockSpec((tm, w), lambda i: (i, 0))
    mem_blk = pl.BlockSpec((MEM_LEN, width), lambda i: (i // tiles_per_batch, 0))
    consts = [p["w_diff_o"], p["w_mla_o"], p["w_cross_o"], p["w_out"], p["g_mlp"], p["w_mlp1"], p["w_mlp2"]]
    return pl.pallas_call(
        _post_kernel,
        grid=(t // tm,),
        in_specs=[row(D_MODEL), row(od.shape[1]), row(om.shape[1]), row(width), mem_blk, mem_blk,
                  row(gates.shape[1])] + [_const_spec(c.shape) for c in consts],
        out_specs=row(D_MODEL),
        out_shape=jax.ShapeDtypeStruct((t, D_MODEL), F32),
        compiler_params=pltpu.CompilerParams(dimension_semantics=("parallel",),
                                             vmem_limit_bytes=VMEM_LIMIT),
        name="post",
    )(x, od, om, xq, ck, cv, gates, *consts)


def _pack_params(w_in, b_gate, g_mix, diff_q_gain, diff_k_gain, diff_lambda, diff_out_gain, w_diff_o,
                 mla_qa_gain, w_mla_qb, mla_kva_gain, w_mla_kvb, mla_q_gain, mla_k_gain, w_mla_o,
                 mem_gain, w_mem_kv, cross_q_gain, cross_k_gain, w_cross_o, w_out, g_mlp, w_mlp1, w_mlp2):
    nl = w_in.shape[0]
    d = D_MODEL
    half = DIFF_HEAD_DIM // 2
    o_dq, o_dk, o_dv = 0, 1024, 2048
    o_cq = 3072
    o_ckv = o_cq + MLA_Q_LORA
    o_kr = o_ckv + MLA_KV_LORA
    o_xq = o_kr + MLA_ROPE_DIM

    def diff_cols(w):
        r = w.shape[1]
        return w.reshape(nl, r, DIFF_HEADS, 2, 2, half // 2, 2).transpose(0, 1, 2, 4, 3, 6, 5).reshape(
            nl, r, DIFF_HEADS * LANES)

    def diff_gain(g, scale):
        g = g.reshape(nl, 2, 1, half // 2, 2).transpose(0, 1, 2, 4, 3)
        g = jnp.broadcast_to(g, (nl, 2, 2, 2, half // 2)).reshape(nl, 1, LANES)
        return (g * scale).astype(F32)

    rh = MLA_ROPE_DIM // 2

    def mla_head_layout(nope, rope):
        pad = jnp.zeros(nope.shape[:-1] + (LANES - MLA_QK_DIM,), nope.dtype)
        return jnp.concatenate([rope[..., :rh], nope[..., :HALF_LANES - rh], rope[..., rh:],
                                nope[..., HALF_LANES - rh:], pad], axis=-1)

    kr_w = w_in[:, :, o_kr:o_xq]
    zpad = jnp.zeros((nl, d, HALF_LANES - rh), F32)
    kr_cols = jnp.concatenate([kr_w[..., :rh], zpad, kr_w[..., rh:], zpad], axis=-1)
    w_in_p = jnp.concatenate([diff_cols(w_in[:, :, o_dq:o_dk]).astype(BF16),
                              diff_cols(w_in[:, :, o_dk:o_dv]).astype(BF16),
                              w_in[:, :, o_dv:o_kr].astype(BF16), kr_cols.astype(BF16),
                              w_in[:, :, o_xq:].astype(BF16)], axis=-1)

    wqb = w_mla_qb.reshape(nl, MLA_Q_LORA, MLA_HEADS, MLA_QK_DIM)
    w_qb_p = mla_head_layout(wqb[..., :MLA_NOPE_DIM], wqb[..., MLA_NOPE_DIM:]).reshape(
        nl, MLA_Q_LORA, MLA_HEADS * LANES).astype(BF16)
    wkvb = w_mla_kvb.reshape(nl, MLA_KV_LORA, MLA_HEADS, MLA_NOPE_DIM + MLA_V_DIM)
    k_nope = wkvb[..., :MLA_NOPE_DIM]
    w_kvk_p = mla_head_layout(k_nope, jnp.zeros(k_nope.shape[:-1] + (MLA_ROPE_DIM,), F32)).reshape(
        nl, MLA_KV_LORA, MLA_HEADS * LANES).astype(BF16)
    w_kvv_p = wkvb[..., MLA_NOPE_DIM:].reshape(nl, MLA_KV_LORA, MLA_HEADS * MLA_V_DIM).astype(BF16)

    def mla_gain(g, scale):
        return (mla_head_layout(g[:, :MLA_NOPE_DIM], g[:, MLA_NOPE_DIM:]) * scale)[:, None, :].astype(F32)

    lam_init = jnp.asarray([0.8 - 0.6 * math.exp(-0.3 * l) for l in range(nl)], F32)
    lam_tab = jnp.zeros((nl, SUBLANES, LANES), F32)
    lam_tab = lam_tab.at[:, 0:4, 0:DIFF_HEAD_DIM].set(diff_lambda.astype(F32))
    lam_tab = lam_tab.at[:, 4, :].set(lam_init[:, None])
    out_gain_rows = jnp.broadcast_to(diff_out_gain[:, :, None], (nl, 2 * DIFF_HEAD_DIM, LANES)).astype(F32)

    row = lambda g: g[:, None, :]
    return dict(
        g_mix=row(g_mix), w_in=w_in_p, b_gate=row(b_gate),
        dq_gain=diff_gain(diff_q_gain, DIFF_HEAD_DIM ** -0.5 * LOG2E), dk_gain=diff_gain(diff_k_gain, 1.0),
        qa_gain=row(mla_qa_gain), w_qb=w_qb_p, kva_gain=row(mla_kva_gain),
        w_kvk=w_kvk_p, w_kvv=w_kvv_p,
        mq_gain=mla_gain(mla_q_gain, MLA_QK_DIM ** -0.5 * LOG2E), mk_gain=mla_gain(mla_k_gain, 1.0),
        xq_gain=row(cross_q_gain * (CROSS_HEAD_DIM ** -0.5 * LOG2E)),
        lam_tab=lam_tab, out_gain_rows=out_gain_rows,
        w_diff_o=w_diff_o.astype(BF16), w_mla_o=w_mla_o.astype(BF16),
        mem_gain=row(mem_gain), w_mem_kv=w_mem_kv.astype(BF16), ck_gain=row(cross_k_gain),
        w_cross_o=w_cross_o.astype(BF16), w_out=w_out.astype(BF16),
        g_mlp=row(g_mlp), w_mlp1=w_mlp1.astype(BF16), w_mlp2=w_mlp2.astype(BF16),
    )


def _rope_lane_tables(positions):
    cos, sin = _rope_tables(positions)
    t = cos.shape[0]
    cosd = jnp.tile(cos, (1, 4))
    sind = jnp.concatenate([-sin, -sin, sin, sin], axis=1)
    rh = MLA_ROPE_DIM // 2
    cos_r, sin_r = cos[:, :rh], sin[:, :rh]
    ones = jnp.ones((t, HALF_LANES - rh), F32)
    zeros = jnp.zeros((t, HALF_LANES - rh), F32)
    cosm = jnp.concatenate([cos_r, ones, cos_r, ones], axis=1)
    sinm = jnp.concatenate([-sin_r, zeros, sin_r, zeros], axis=1)
    return cosd, sind, cosm, sinm


def kernel(x, mem, positions, g_mix, w_in, b_gate, diff_q_gain, diff_k_gain, diff_lambda, diff_out_gain,
           w_diff_o, mla_qa_gain, w_mla_qb, mla_kva_gain, w_mla_kvb, mla_q_gain, mla_k_gain, w_mla_o,
           mem_gain, w_mem_kv, cross_q_gain, cross_k_gain, w_cross_o, w_out, g_mlp, w_mlp1, w_mlp2):
    batch, seq, d = x.shape
    tabs = _rope_lane_tables(positions)
    packed = _pack_params(w_in, b_gate, g_mix, diff_q_gain, diff_k_gain, diff_lambda, diff_out_gain,
                          w_diff_o, mla_qa_gain, w_mla_qb, mla_kva_gain, w_mla_kvb, mla_q_gain, mla_k_gain,
                          w_mla_o, mem_gain, w_mem_kv, cross_q_gain, cross_k_gain, w_cross_o, w_out,
                          g_mlp, w_mlp1, w_mlp2)
    xt = x.reshape(batch * seq, d)
    mem2d = mem.reshape(batch * mem.shape[1], d)
    for l in range(w_in.shape[0]):
        p = {name: a[l] for name, a in packed.items()}
        dq, dk, dv, mq, mk, mv, xq, gates = _inproj(xt, tabs, p)
        od = _diff_attn(p["lam_tab"], p["out_gain_rows"], dq, dk, dv, batch, seq)
        om = _mla_attn(mq, mk, mv, batch, seq)
        ck, cv = _mem_kv(mem2d, p)
        xt = _post(xt, od, om, xq, ck, cv, gates, p, seq)
    return xt.reshape(batch, seq, d)
```

```python
import math

import jax
import jax.numpy as jnp
from jax import lax
from jax.experimental import pallas as pl
from jax.experimental.pallas import tpu as pltpu

D_MODEL = 1024
DEPTH = 4
MEM_LEN = 256
ROPE_THETA = 10000.0
EPS = 1e-6

DIFF_HEADS = 8
DIFF_HEAD_DIM = 64
MLA_HEADS = 8
MLA_Q_LORA = 384
MLA_KV_LORA = 256
MLA_NOPE_DIM = 64
MLA_ROPE_DIM = 32
MLA_V_DIM = 64
MLA_QK_DIM = MLA_NOPE_DIM + MLA_ROPE_DIM
CROSS_HEADS = 4
CROSS_HEAD_DIM = 128
D_FF = 4 * D_MODEL

LANES = 128
SUBLANES = 8
HALF_LANES = LANES // 2
LOG2E = math.log2(math.e)

C_DQ = 0
C_DK = C_DQ + DIFF_HEADS * LANES
C_DV = C_DK + DIFF_HEADS * LANES
C_CQ = C_DV + DIFF_HEADS * LANES
C_CKV = C_CQ + MLA_Q_LORA
C_KR = C_CKV + MLA_KV_LORA
C_XQ = C_KR + LANES
C_GATE = C_XQ + CROSS_HEADS * CROSS_HEAD_DIM
IN_WIDTH_PACKED = C_GATE + 3 * D_MODEL

TM_IN = 512
TM_IN_SUB = 512
TM_OUT = 512
TQ = 256
ATTN_GROUPS = 2
SM_ROWS = 16
VMEM_LIMIT = 56 * 1024 * 1024

BF16 = jnp.bfloat16
F32 = jnp.float32


def _const_spec(shape):
    nd = len(shape)
    return pl.BlockSpec(shape, lambda *_: (0,) * nd, pipeline_mode=pl.Buffered(1))


def _dot(a, b):
    return jnp.dot(a, b, preferred_element_type=F32)


def _dot_nt(a, b):
    return lax.dot_general(a, b, (((1,), (1,)), ((), ())), preferred_element_type=F32)


def _inv_rms(y, n):
    return lax.rsqrt(jnp.sum(y * y, axis=-1, keepdims=True) * (1.0 / n) + EPS)


def _rope64(y, cos, sin):
    return y * cos + pltpu.roll(y, HALF_LANES, axis=1) * sin


def _rope_table_kernel(pos_ref, invf_ref, cos_ref, sin_ref):
    ang = pos_ref[...].astype(F32) * invf_ref[...]
    cos_ref[...] = jnp.cos(ang)
    sin_ref[...] = jnp.sin(ang)


def _rope_tables(positions):
    nfreq = DIFF_HEAD_DIM // 2
    per_row = LANES // nfreq
    t = positions.size
    inv_freq = ROPE_THETA ** (-jnp.arange(0, DIFF_HEAD_DIM, 2, dtype=F32) / DIFF_HEAD_DIM)
    inv_freq = jnp.concatenate([inv_freq[0::2], inv_freq[1::2]])
    pos_rep = jnp.repeat(positions.reshape(t // per_row, per_row), nfreq, axis=1)
    invf = jnp.tile(inv_freq, per_row)[None, :]
    rows = t // per_row
    tr = min(2048, rows)
    cos, sin = pl.pallas_call(
        _rope_table_kernel,
        grid=(rows // tr,),
        in_specs=[pl.BlockSpec((tr, LANES), lambda i: (i, 0)),
                  pl.BlockSpec((1, LANES), lambda i: (0, 0))],
        out_specs=[pl.BlockSpec((tr, LANES), lambda i: (i, 0))] * 2,
        out_shape=[jax.ShapeDtypeStruct((rows, LANES), F32)] * 2,
        name="rope_tables",
    )(pos_rep, invf)
    return cos.reshape(t, nfreq), sin.reshape(t, nfreq)


def _inproj_kernel(x_ref, cosd_ref, sind_ref, cosm_ref, sinm_ref,
                   gmix_ref, win_ref, bgate_ref, dqg_ref, dkg_ref,
                   qag_ref, wqb_ref, kvag_ref, wkvk_ref, wkvv_ref, mqg_ref, mkg_ref, xqg_ref,
                   dq_ref, dk_ref, dv_ref, mq_ref, mk_ref, mv_ref, xq_ref, gate_ref):
    for r0 in range(0, x_ref.shape[0], TM_IN_SUB):
        _inproj_rows(slice(r0, r0 + TM_IN_SUB), x_ref, cosd_ref, sind_ref, cosm_ref, sinm_ref,
                     gmix_ref, win_ref, bgate_ref, dqg_ref, dkg_ref,
                     qag_ref, wqb_ref, kvag_ref, wkvk_ref, wkvv_ref, mqg_ref, mkg_ref, xqg_ref,
                     dq_ref, dk_ref, dv_ref, mq_ref, mk_ref, mv_ref, xq_ref, gate_ref)


def _inproj_rows(rs, x_ref, cosd_ref, sind_ref, cosm_ref, sinm_ref,
                 gmix_ref, win_ref, bgate_ref, dqg_ref, dkg_ref,
                 qag_ref, wqb_ref, kvag_ref, wkvk_ref, wkvv_ref, mqg_ref, mkg_ref, xqg_ref,
                 dq_ref, dk_ref, dv_ref, mq_ref, mk_ref, mv_ref, xq_ref, gate_ref):
    x = x_ref[rs, :]
    h = (x * _inv_rms(x, D_MODEL) * gmix_ref[...]).astype(BF16)
    tm = x.shape[0]

    cosd, sind = cosd_ref[rs, :], sind_ref[rs, :]
    cosm, sinm = cosm_ref[rs, :], sinm_ref[rs, :]
    lane = lax.broadcasted_iota(jnp.int32, (tm, LANES), 1)
    is_map0 = (lane & 32) == 0

    def diff_qk(col0, gain_ref, out_ref):
        y = _dot(h, win_ref[:, col0:col0 + DIFF_HEADS * LANES])
        gain = gain_ref[...]
        for hd in range(DIFF_HEADS):
            yh = y[:, hd * LANES:(hd + 1) * LANES]
            sq = yh * yh
            s_all = jnp.sum(sq, axis=-1, keepdims=True)
            s0 = jnp.sum(jnp.where(is_map0, sq, 0.0), axis=-1, keepdims=True)
            r0 = lax.rsqrt(s0 * (1.0 / DIFF_HEAD_DIM) + EPS)
            r1 = lax.rsqrt((s_all - s0) * (1.0 / DIFF_HEAD_DIM) + EPS)
            yn = yh * jnp.where(is_map0, r0, r1) * gain
            out_ref[rs, hd * LANES:(hd + 1) * LANES] = _rope64(yn, cosd, sind).astype(BF16)

    def diff_v():
        dv_ref[rs, :] = _dot(h, win_ref[:, C_DV:C_DV + DIFF_HEADS * LANES]).astype(BF16)

    def gate(br):
        sl = slice(br * D_MODEL, (br + 1) * D_MODEL)
        z = _dot(h, win_ref[:, C_GATE + br * D_MODEL:C_GATE + (br + 1) * D_MODEL]) + bgate_ref[:, sl]
        gate_ref[rs, sl] = (1.0 / (1.0 + jnp.exp(-z))).astype(BF16)

    diff_qk(C_DQ, dqg_ref, dq_ref)
    diff_qk(C_DK, dkg_ref, dk_ref)
    diff_v()

    lat = _dot(h, win_ref[:, C_CQ:C_XQ])
    cq = lat[:, 0:MLA_Q_LORA]
    ckv = lat[:, MLA_Q_LORA:MLA_Q_LORA + MLA_KV_LORA]
    kr = lat[:, MLA_Q_LORA + MLA_KV_LORA:]
    cqn = (cq * _inv_rms(cq, MLA_Q_LORA) * qag_ref[...]).astype(BF16)
    ckvn = (ckv * _inv_rms(ckv, MLA_KV_LORA) * kvag_ref[...]).astype(BF16)
    kr = _rope64(kr, cosm, sinm)

    q = _dot(cqn, wqb_ref[...])
    kn = _dot(ckvn, wkvk_ref[...])
    mv_ref[rs, :] = _dot(ckvn, wkvv_ref[...]).astype(BF16)
    mqg, mkg = mqg_ref[...], mkg_ref[...]
    for hd in range(MLA_HEADS):
        sl = slice(hd * LANES, (hd + 1) * LANES)
        qh = _rope64(q[:, sl], cosm, sinm)
        mq_ref[rs, sl] = (qh * _inv_rms(qh, MLA_QK_DIM) * mqg).astype(BF16)
        kh = kn[:, sl] + kr
        mk_ref[rs, sl] = (kh * _inv_rms(kh, MLA_QK_DIM) * mkg).astype(BF16)

    xq = _dot(h, win_ref[:, C_XQ:C_GATE])
    xqg = xqg_ref[...]
    for hd in range(CROSS_HEADS):
        sl = slice(hd * LANES, (hd + 1) * LANES)
        yh = xq[:, sl]
        xq_ref[rs, sl] = (yh * _inv_rms(yh, CROSS_HEAD_DIM) * xqg).astype(BF16)
    for br in range(3):
        gate(br)


def _inproj(x, tabs, p):
    t = x.shape[0]
    tm = min(TM_IN, t)
    row = lambda w: pl.BlockSpec((tm, w), lambda i: (i, 0))
    consts = [p["g_mix"], p["w_in"], p["b_gate"], p["dq_gain"], p["dk_gain"],
              p["qa_gain"], p["w_qb"], p["kva_gain"], p["w_kvk"], p["w_kvv"],
              p["mq_gain"], p["mk_gain"], p["xq_gain"]]
    out_widths = [DIFF_HEADS * LANES] * 3 + [MLA_HEADS * LANES] * 2 + [MLA_HEADS * MLA_V_DIM,
                                                                      CROSS_HEADS * CROSS_HEAD_DIM, 3 * D_MODEL]
    return pl.pallas_call(
        _inproj_kernel,
        grid=(t // tm,),
        in_specs=[row(D_MODEL)] + [row(LANES)] * 4 + [_const_spec(c.shape) for c in consts],
        out_specs=[row(w) for w in out_widths],
        out_shape=[jax.ShapeDtypeStruct((t, w), BF16) for w in out_widths],
        compiler_params=pltpu.CompilerParams(dimension_semantics=("parallel",),
                                             vmem_limit_bytes=VMEM_LIMIT),
        name="inproj",
    )(x, *tabs, *consts)


def _causal_bias_t(tq):
    r = lax.broadcasted_iota(jnp.int32, (tq, tq), 0)
    c = lax.broadcasted_iota(jnp.int32, (tq, tq), 1)
    return jnp.where(r <= c, 0.0, -jnp.inf).astype(F32)


def _fold_rows(a, op):
    n, w = a.shape
    return op(op(a.reshape(n // SUBLANES, SUBLANES, w), axis=0), axis=0, keepdims=True)


def _causal_attention(jobs, tq, k_ref, s_ref, p_ref):
    bias_t = _causal_bias_t(tq)
    n_jobs = len(jobs)

    def scores(j):
        i, make_kq, _, _ = jobs[j]
        lo, hi = i * tq, (i + 1) * tq
        s_buf = s_ref.at[j % 2]
        maxima = []
        for k_sl, q_rows, s_sl in make_kq():
            reps = q_rows.shape[0] // tq
            bias = bias_t if reps == 1 else jnp.concatenate([bias_t] * reps, axis=1)
            s = _dot_nt(k_ref[lo:hi, k_sl], q_rows) + bias
            s_buf[lo:hi, s_sl] = s
            m = _fold_rows(s, jnp.max)
            if lo > 0:
                s = _dot_nt(k_ref[0:lo, k_sl], q_rows)
                s_buf[0:lo, s_sl] = s
                m = jnp.maximum(m, _fold_rows(s, jnp.max))
            maxima.append(m)
        return maxima[0] if len(maxima) == 1 else jnp.concatenate(maxima, axis=1)

    def softmax(j, m):
        hi = (jobs[j][0] + 1) * tq
        s_buf, p_buf = s_ref.at[j % 2], p_ref.at[j % 2]
        for r in range(0, hi, SM_ROWS):
            p_buf[r:r + SM_ROWS, :] = jnp.exp2(s_buf[r:r + SM_ROWS, :] - m).astype(BF16)

    def pv(j):
        i, _, vt_ref, finish = jobs[j]
        hi = (i + 1) * tq
        acc = _dot(vt_ref[:, 0:hi], p_ref[j % 2, 0:hi, :])
        finish(acc[0:LANES], acc[LANES:LANES + 1])

    m_next = scores(0)
    for j in range(n_jobs):
        m_cur = m_next
        if j + 1 < n_jobs:
            m_next = scores(j + 1)
        softmax(j, m_cur)
        if j > 0:
            pv(j - 1)
    pv(n_jobs - 1)


VT_ROWS = LANES + 16


def _store_vt(v_ref, vt_ref):
    for g in range(vt_ref.shape[0]):
        vt_ref[g, 0:LANES, :] = v_ref[:, g * LANES:(g + 1) * LANES].astype(F32).T.astype(BF16)
        vt_ref[g, LANES:VT_ROWS, :] = jnp.ones((VT_ROWS - LANES, vt_ref.shape[2]), BF16)


def _diff_attn_kernel(lam_ref, og_ref, q_ref, k_ref, v_ref, o_ref, vt_ref, s_ref, p_ref):
    s_len = q_ref.shape[0]
    tq = TQ
    lane = lax.broadcasted_iota(jnp.int32, (tq, LANES), 1)
    is_map0 = (lane & 32) == 0
    _store_vt(v_ref, vt_ref)

    lv = lam_ref[...]
    lam = (jnp.exp(jnp.sum(lv[0:1] * lv[1:2], axis=-1, keepdims=True))
           - jnp.exp(jnp.sum(lv[2:3] * lv[3:4], axis=-1, keepdims=True))
           + lv[4:5, 0:1])
    og = og_ref[...] * (1.0 - lv[4:5, 0:1])
    og = jnp.concatenate([og] * (tq // LANES), axis=1)

    def job(g, i):
        cols = slice(g * LANES, (g + 1) * LANES)
        rows = slice(i * tq, (i + 1) * tq)

        def make_kq():
            q = q_ref[rows, cols]
            zero = jnp.zeros_like(q)
            qq = jnp.concatenate([jnp.where(is_map0, q, zero), jnp.where(is_map0, zero, q)], axis=0)
            return [(cols, qq, slice(None))]

        def finish(acc, l):
            o = acc / l
            o = o[:, 0:tq] - lam * o[:, tq:2 * tq]
            r = lax.rsqrt(jnp.sum(o * o, axis=0, keepdims=True) * (1.0 / (2 * DIFF_HEAD_DIM)) + EPS)
            o_ref[rows, cols] = (o * r * og).T.astype(BF16)

        return i, make_kq, vt_ref.at[g], finish

    jobs = [job(g, i) for g in range(vt_ref.shape[0]) for i in range(s_len // tq)]
    _causal_attention(jobs, tq, k_ref, s_ref, p_ref)


def _attn_scratch(seq):
    return [pltpu.VMEM((ATTN_GROUPS, VT_ROWS, seq), BF16),
            pltpu.VMEM((2, seq, 2 * TQ), F32),
            pltpu.VMEM((2, seq, 2 * TQ), BF16)]


def _diff_attn(lam_tab, out_gain_rows, dq, dk, dv, batch, seq):
    blk = pl.BlockSpec((None, seq, ATTN_GROUPS * LANES), lambda b, h: (b, 0, h))
    shp = (batch, seq, DIFF_HEADS * LANES)
    out = pl.pallas_call(
        _diff_attn_kernel,
        grid=(batch, DIFF_HEADS // ATTN_GROUPS),
        in_specs=[_const_spec(lam_tab.shape), _const_spec(out_gain_rows.shape), blk, blk, blk],
        out_specs=blk,
        out_shape=jax.ShapeDtypeStruct(shp, BF16),
        scratch_shapes=_attn_scratch(seq),
        compiler_params=pltpu.CompilerParams(dimension_semantics=("parallel", "parallel"),
                                             vmem_limit_bytes=VMEM_LIMIT),
        name="diff_attn",
    )(lam_tab, out_gain_rows, dq.reshape(shp), dk.reshape(shp), dv.reshape(shp))
    return out.reshape(batch * seq, DIFF_HEADS * LANES)


def _mla_attn_kernel(q_ref, k_ref, v_ref, o_ref, vt_ref, s_ref, p_ref):
    s_len = q_ref.shape[0]
    tq = TQ
    _store_vt(v_ref, vt_ref)
    row = lax.broadcasted_iota(jnp.int32, (LANES, tq), 0)
    is_head0 = row < MLA_V_DIM

    def job(g, i):
        rows = slice(i * tq, (i + 1) * tq)

        def make_kq():
            heads = [slice((2 * g + hd) * LANES, (2 * g + hd + 1) * LANES) for hd in range(2)]
            return [(heads[hd], q_ref[rows, heads[hd]], slice(hd * tq, (hd + 1) * tq)) for hd in range(2)]

        def finish(acc, l):
            o = acc / l
            o = jnp.where(is_head0, o[:, 0:tq], o[:, tq:2 * tq])
            o_ref[rows, g * LANES:(g + 1) * LANES] = o.T.astype(BF16)

        return i, make_kq, vt_ref.at[g], finish

    jobs = [job(g, i) for g in range(vt_ref.shape[0]) for i in range(s_len // tq)]
    _causal_attention(jobs, tq, k_ref, s_ref, p_ref)


def _mla_attn(mq, mk, mv, batch, seq):
    pairs = ATTN_GROUPS
    qk_blk = pl.BlockSpec((None, seq, pairs * 2 * LANES), lambda b, h: (b, 0, h))
    v_blk = pl.BlockSpec((None, seq, pairs * 2 * MLA_V_DIM), lambda b, h: (b, 0, h))
    qk_shp = (batch, seq, MLA_HEADS * LANES)
    v_shp = (batch, seq, MLA_HEADS * MLA_V_DIM)
    out = pl.pallas_call(
        _mla_attn_kernel,
        grid=(batch, MLA_HEADS // (2 * pairs)),
        in_specs=[qk_blk, qk_blk, v_blk],
        out_specs=v_blk,
        out_shape=jax.ShapeDtypeStruct(v_shp, BF16),
        scratch_shapes=_attn_scratch(seq),
        compiler_params=pltpu.CompilerParams(dimension_semantics=("parallel", "parallel"),
                                             vmem_limit_bytes=VMEM_LIMIT),
        name="mla_attn",
    )(mq.reshape(qk_shp), mk.reshape(qk_shp), mv.reshape(v_shp))
    return out.reshape(batch * seq, MLA_HEADS * MLA_V_DIM)


def _mem_kv_kernel(mem_ref, mg_ref, w_ref, kg_ref, k_ref, v_ref):
    m = mem_ref[...]
    mn = (m * _inv_rms(m, D_MODEL) * mg_ref[...]).astype(BF16)
    kv = _dot(mn, w_ref[...])
    width = CROSS_HEADS * CROSS_HEAD_DIM
    kg = kg_ref[...]
    for hd in range(CROSS_HEADS):
        sl = slice(hd * LANES, (hd + 1) * LANES)
        kh = kv[:, sl]
        k_ref[:, sl] = (kh * _inv_rms(kh, CROSS_HEAD_DIM) * kg).astype(BF16)
    v_ref[...] = kv[:, width:].astype(BF16)


def _mem_kv(mem2d, p):
    rows = mem2d.shape[0]
    width = CROSS_HEADS * CROSS_HEAD_DIM
    tm = min(512, rows)
    return pl.pallas_call(
        _mem_kv_kernel,
        grid=(rows // tm,),
        in_specs=[pl.BlockSpec((tm, D_MODEL), lambda i: (i, 0)), _const_spec(p["mem_gain"].shape),
                  _const_spec(p["w_mem_kv"].shape), _const_spec(p["ck_gain"].shape)],
        out_specs=[pl.BlockSpec((tm, width), lambda i: (i, 0))] * 2,
        out_shape=[jax.ShapeDtypeStruct((rows, width), BF16)] * 2,
        compiler_params=pltpu.CompilerParams(dimension_semantics=("parallel",),
                                             vmem_limit_bytes=VMEM_LIMIT),
        name="mem_kv",
    )(mem2d, p["mem_gain"], p["w_mem_kv"], p["ck_gain"])


FF_CHUNK = 1024


def _post_kernel(x_ref, od_ref, om_ref, xq_ref, ck_ref, cv_ref, g_ref,
                 wdo_ref, wmo_ref, wco_ref, wout_ref, gmlp_ref, w1_ref, w2_ref, o_ref):
    heads = []
    for hd in range(CROSS_HEADS):
        sl = slice(hd * LANES, (hd + 1) * LANES)
        s = _dot_nt(xq_ref[:, sl], ck_ref[:, sl])
        p = jnp.exp2(s - jnp.max(s, axis=-1, keepdims=True))
        l = jnp.sum(p, axis=-1, keepdims=True)
        heads.append((_dot(p.astype(BF16), cv_ref[:, sl]) / l).astype(BF16))
    oc = jnp.concatenate(heads, axis=1)

    merged = (g_ref[:, 0:D_MODEL].astype(F32) * _dot(od_ref[...], wdo_ref[...])
              + g_ref[:, D_MODEL:2 * D_MODEL].astype(F32) * _dot(om_ref[...], wmo_ref[...])
              + g_ref[:, 2 * D_MODEL:].astype(F32) * _dot(oc, wco_ref[...]))
    x = x_ref[...] + _dot(merged.astype(BF16), wout_ref[...])

    h = (x * _inv_rms(x, D_MODEL) * gmlp_ref[...]).astype(BF16)
    y = x
    for c in range(D_FF // FF_CHUNK):
        sl = slice(c * FF_CHUNK, (c + 1) * FF_CHUNK)
        u = jnp.maximum(_dot(h, w1_ref[:, sl]), 0.0)
        y = y + _dot((u * u).astype(BF16), w2_ref[sl, :])
    o_ref[...] = y


def _post(x, od, om, xq, ck, cv, gates, p, seq):
    t = x.shape[0]
    tm = min(TM_OUT, seq)
    tiles_per_batch = seq // tm
    width = CROSS_HEADS * CROSS_HEAD_DIM
    row = lambda w: pl.BlockSpec((tm, w), lambda i: (i, 0))
    mem_blk = pl.BlockSpec((MEM_LEN, width), lambda i: (i // tiles_per_batch, 0))
    consts = [p["w_diff_o"], p["w_mla_o"], p["w_cross_o"], p["w_out"], p["g_mlp"], p["w_mlp1"], p["w_mlp2"]]
    return pl.pallas_call(
        _post_kernel,
        grid=(t // tm,),
        in_specs=[row(D_MODEL), row(od.shape[1]), row(om.shape[1]), row(width), mem_blk, mem_blk,
                  row(gates.shape[1])] + [_const_spec(c.shape) for c in consts],
        out_specs=row(D_MODEL),
        out_shape=jax.ShapeDtypeStruct((t, D_MODEL), F32),
        compiler_params=pltpu.CompilerParams(dimension_semantics=("parallel",),
                                             vmem_limit_bytes=VMEM_LIMIT),
        name="post",
    )(x, od, om, xq, ck, cv, gates, *consts)


def _pack_params(w_in, b_gate, g_mix, diff_q_gain, diff_k_gain, diff_lambda, diff_out_gain, w_diff_o,
                 mla_qa_gain, w_mla_qb, mla_kva_gain, w_mla_kvb, mla_q_gain, mla_k_gain, w_mla_o,
                 mem_gain, w_mem_kv, cross_q_gain, cross_k_gain, w_cross_o, w_out, g_mlp, w_mlp1, w_mlp2):
    nl = w_in.shape[0]
    d = D_MODEL
    half = DIFF_HEAD_DIM // 2
    o_dq, o_dk, o_dv = 0, 1024, 2048
    o_cq = 3072
    o_ckv = o_cq + MLA_Q_LORA
    o_kr = o_ckv + MLA_KV_LORA
    o_xq = o_kr + MLA_ROPE_DIM

    def diff_cols(w):
        r = w.shape[1]
        return w.reshape(nl, r, DIFF_HEADS, 2, 2, half // 2, 2).transpose(0, 1, 2, 4, 3, 6, 5).reshape(
            nl, r, DIFF_HEADS * LANES)

    def diff_gain(g, scale):
        g = g.reshape(nl, 2, 1, half // 2, 2).transpose(0, 1, 2, 4, 3)
        g = jnp.broadcast_to(g, (nl, 2, 2, 2, half // 2)).reshape(nl, 1, LANES)
        return (g * scale).astype(F32)

    rh = MLA_ROPE_DIM // 2

    def mla_head_layout(nope, rope):
        pad = jnp.zeros(nope.shape[:-1] + (LANES - MLA_QK_DIM,), nope.dtype)
        return jnp.concatenate([rope[..., :rh], nope[..., :HALF_LANES - rh], rope[..., rh:],
                                nope[..., HALF_LANES - rh:], pad], axis=-1)

    kr_w = w_in[:, :, o_kr:o_xq]
    zpad = jnp.zeros((nl, d, HALF_LANES - rh), F32)
    kr_cols = jnp.concatenate([kr_w[..., :rh], zpad, kr_w[..., rh:], zpad], axis=-1)
    w_in_p = jnp.concatenate([diff_cols(w_in[:, :, o_dq:o_dk]).astype(BF16),
                              diff_cols(w_in[:, :, o_dk:o_dv]).astype(BF16),
                              w_in[:, :, o_dv:o_kr].astype(BF16), kr_cols.astype(BF16),
                              w_in[:, :, o_xq:].astype(BF16)], axis=-1)

    wqb = w_mla_qb.reshape(nl, MLA_Q_LORA, MLA_HEADS, MLA_QK_DIM)
    w_qb_p = mla_head_layout(wqb[..., :MLA_NOPE_DIM], wqb[..., MLA_NOPE_DIM:]).reshape(
        nl, MLA_Q_LORA, MLA_HEADS * LANES).astype(BF16)
    wkvb = w_mla_kvb.reshape(nl, MLA_KV_LORA, MLA_HEADS, MLA_NOPE_DIM + MLA_V_DIM)
    k_nope = wkvb[..., :MLA_NOPE_DIM]
    w_kvk_p = mla_head_layout(k_nope, jnp.zeros(k_nope.shape[:-1] + (MLA_ROPE_DIM,), F32)).reshape(
        nl, MLA_KV_LORA, MLA_HEADS * LANES).astype(BF16)
    w_kvv_p = wkvb[..., MLA_NOPE_DIM:].reshape(nl, MLA_KV_LORA, MLA_HEADS * MLA_V_DIM).astype(BF16)

    def mla_gain(g, scale):
        return (mla_head_layout(g[:, :MLA_NOPE_DIM], g[:, MLA_NOPE_DIM:]) * scale)[:, None, :].astype(F32)

    lam_init = jnp.asarray([0.8 - 0.6 * math.exp(-0.3 * l) for l in range(nl)], F32)
    lam_tab = jnp.zeros((nl, SUBLANES, LANES), F32)
    lam_tab = lam_tab.at[:, 0:4, 0:DIFF_HEAD_DIM].set(diff_lambda.astype(F32))
    lam_tab = lam_tab.at[:, 4, :].set(lam_init[:, None])
    out_gain_rows = jnp.broadcast_to(diff_out_gain[:, :, None], (nl, 2 * DIFF_HEAD_DIM, LANES)).astype(F32)

    row = lambda g: g[:, None, :]
    return dict(
        g_mix=row(g_mix), w_in=w_in_p, b_gate=row(b_gate),
        dq_gain=diff_gain(diff_q_gain, DIFF_HEAD_DIM ** -0.5 * LOG2E), dk_gain=diff_gain(diff_k_gain, 1.0),
        qa_gain=row(mla_qa_gain), w_qb=w_qb_p, kva_gain=row(mla_kva_gain),
        w_kvk=w_kvk_p, w_kvv=w_kvv_p,
        mq_gain=mla_gain(mla_q_gain, MLA_QK_DIM ** -0.5 * LOG2E), mk_gain=mla_gain(mla_k_gain, 1.0),
        xq_gain=row(cross_q_gain * (CROSS_HEAD_DIM ** -0.5 * LOG2E)),
        lam_tab=lam_tab, out_gain_rows=out_gain_rows,
        w_diff_o=w_diff_o.astype(BF16), w_mla_o=w_mla_o.astype(BF16),
        mem_gain=row(mem_gain), w_mem_kv=w_mem_kv.astype(BF16), ck_gain=row(cross_k_gain),
        w_cross_o=w_cross_o.astype(BF16), w_out=w_out.astype(BF16),
        g_mlp=row(g_mlp), w_mlp1=w_mlp1.astype(BF16), w_mlp2=w_mlp2.astype(BF16),
    )


def _rope_lane_tables(positions):
    cos, sin = _rope_tables(positions)
    t = cos.shape[0]
    cosd = jnp.tile(cos, (1, 4))
    sind = jnp.concatenate([-sin, -sin, sin, sin], axis=1)
    rh = MLA_ROPE_DIM // 2
    cos_r, sin_r = cos[:, :rh], sin[:, :rh]
    ones = jnp.ones((t, HALF_LANES - rh), F32)
    zeros = jnp.zeros((t, HALF_LANES - rh), F32)
    cosm = jnp.concatenate([cos_r, ones, cos_r, ones], axis=1)
    sinm = jnp.concatenate([-sin_r, zeros, sin_r, zeros], axis=1)
    return cosd, sind, cosm, sinm


def kernel(x, mem, positions, g_mix, w_in, b_gate, diff_q_gain, diff_k_gain, diff_lambda, diff_out_gain,
           w_diff_o, mla_qa_gain, w_mla_qb, mla_kva_gain, w_mla_kvb, mla_q_gain, mla_k_gain, w_mla_o,
           mem_gain, w_mem_kv, cross_q_gain, cross_k_gain, w_cross_o, w_out, g_mlp, w_mlp1, w_mlp2):
    batch, seq, d = x.shape
    tabs = _rope_lane_tables(positions)
    packed = _pack_params(w_in, b_gate, g_mix, diff_q_gain, diff_k_gain, diff_lambda, diff_out_gain,
                          w_diff_o, mla_qa_gain, w_mla_qb, mla_kva_gain, w_mla_kvb, mla_q_gain, mla_k_gain,
                          w_mla_o, mem_gain, w_mem_kv, cross_q_gain, cross_k_gain, w_cross_o, w_out,
                          g_mlp, w_mlp1, w_mlp2)
    xt = x.reshape(batch * seq, d)
    mem2d = mem.reshape(batch * mem.shape[1], d)
    for l in range(w_in.shape[0]):
        p = {name: a[l] for name, a in packed.items()}
        dq, dk, dv, mq, mk, mv, xq, gates = _inproj(xt, tabs, p)
        od = _diff_attn(p["lam_tab"], p["out_gain_rows"], dq, dk, dv, batch, seq)
        om = _mla_attn(mq, mk, mv, batch, seq)
        ck, cv = _mem_kv(mem2d, p)
        xt = _post(xt, od, om, xq, ck, cv, gates, p, seq)
    return xt.reshape(batch, seq, d)
```

```python
import math

import jax
import jax.numpy as jnp
from jax import lax
from jax.experimental import pallas as pl
from jax.experimental.pallas import tpu as pltpu

D_MODEL = 1024
DEPTH = 4
MEM_LEN = 256
ROPE_THETA = 10000.0
EPS = 1e-6

DIFF_HEADS = 8
DIFF_HEAD_DIM = 64
MLA_HEADS = 8
MLA_Q_LORA = 384
MLA_KV_LORA = 256
MLA_NOPE_DIM = 64
MLA_ROPE_DIM = 32
MLA_V_DIM = 64
MLA_QK_DIM = MLA_NOPE_DIM + MLA_ROPE_DIM
CROSS_HEADS = 4
CROSS_HEAD_DIM = 128
D_FF = 4 * D_MODEL

LANES = 128
SUBLANES = 8
HALF_LANES = LANES // 2
LOG2E = math.log2(math.e)

C_DQ = 0
C_DK = C_DQ + DIFF_HEADS * LANES
C_DV = C_DK + DIFF_HEADS * LANES
C_CQ = C_DV + DIFF_HEADS * LANES
C_CKV = C_CQ + MLA_Q_LORA
C_KR = C_CKV + MLA_KV_LORA
C_XQ = C_KR + LANES
C_GATE = C_XQ + CROSS_HEADS * CROSS_HEAD_DIM
IN_WIDTH_PACKED = C_GATE + 3 * D_MODEL

TM_IN = 512
TM_OUT = 512
TQ = 256
ATTN_GROUPS = 2
SM_ROWS = 16
VMEM_LIMIT = 56 * 1024 * 1024

BF16 = jnp.bfloat16
F32 = jnp.float32


def _const_spec(shape):
    nd = len(shape)
    return pl.BlockSpec(shape, lambda *_: (0,) * nd, pipeline_mode=pl.Buffered(1))


def _dot(a, b):
    return jnp.dot(a, b, preferred_element_type=F32)


def _dot_nt(a, b):
    return lax.dot_general(a, b, (((1,), (1,)), ((), ())), preferred_element_type=F32)


def _inv_rms(y, n):
    return lax.rsqrt(jnp.sum(y * y, axis=-1, keepdims=True) * (1.0 / n) + EPS)


def _rope64(y, cos, sin):
    return y * cos + pltpu.roll(y, HALF_LANES, axis=1) * sin


ROPE_FREQS = DIFF_HEAD_DIM // 2
ROPE_GROUPS = LANES // ROPE_FREQS


def _rope_table_kernel(pos_ref, invf_ref, cos_ref, sin_ref):
    ang = pos_ref[...].astype(F32) * invf_ref[...]
    cos, sin = jnp.cos(ang), jnp.sin(ang)
    lane = lax.broadcasted_iota(jnp.int32, ang.shape, 1)
    sign = jnp.where(lane < HALF_LANES, -1.0, 1.0)

    def spread(x, k):
        if k:
            x = pltpu.roll(x, LANES - ROPE_FREQS * k, axis=1)
        x = jnp.where(lane < ROPE_FREQS, x, pltpu.roll(x, ROPE_FREQS, axis=1))
        return jnp.where(lane < HALF_LANES, x, pltpu.roll(x, HALF_LANES, axis=1))

    for k in range(ROPE_GROUPS):
        cos_ref[k] = spread(cos, k)
        sin_ref[k] = spread(sin, k) * sign


def _rope_tables(positions):
    t = positions.size
    rows = t // ROPE_GROUPS
    inv_freq = ROPE_THETA ** (-jnp.arange(0, DIFF_HEAD_DIM, 2, dtype=F32) / DIFF_HEAD_DIM)
    inv_freq = jnp.concatenate([inv_freq[0::2], inv_freq[1::2]])
    pos_rep = jnp.repeat(positions.reshape(ROPE_GROUPS, rows).T, ROPE_FREQS, axis=1)
    invf = jnp.tile(inv_freq, ROPE_GROUPS)[None, :]
    tr = min(1024, rows)
    out_blk = pl.BlockSpec((ROPE_GROUPS, tr, LANES), lambda i: (0, i, 0))
    cos, sin = pl.pallas_call(
        _rope_table_kernel,
        grid=(rows // tr,),
        in_specs=[pl.BlockSpec((tr, LANES), lambda i: (i, 0)),
                  pl.BlockSpec((1, LANES), lambda i: (0, 0))],
        out_specs=[out_blk] * 2,
        out_shape=[jax.ShapeDtypeStruct((ROPE_GROUPS, rows, LANES), F32)] * 2,
        name="rope_tables",
    )(pos_rep, invf)
    return cos.reshape(t, LANES), sin.reshape(t, LANES)


def _inproj_kernel(x_ref, cosd_ref, sind_ref,
                   gmix_ref, win_ref, bgate_ref, dqg_ref, dkg_ref,
                   qag_ref, wqb_ref, kvag_ref, wkvk_ref, wkvv_ref, mqg_ref, mkg_ref, xqg_ref,
                   dq_ref, dk_ref, dv_ref, mq_ref, mk_ref, mv_ref, xq_ref, gate_ref):
    x = x_ref[...]
    h = (x * _inv_rms(x, D_MODEL) * gmix_ref[...]).astype(BF16)
    tm = x.shape[0]
    rs = slice(None)

    cosd, sind = cosd_ref[...], sind_ref[...]
    lane = lax.broadcasted_iota(jnp.int32, (tm, LANES), 1)
    is_map0 = (lane & 32) == 0
    is_mla_rope = (lane & (HALF_LANES - MLA_ROPE_DIM // 2)) == 0
    cosm = jnp.where(is_mla_rope, cosd, 1.0)
    sinm = jnp.where(is_mla_rope, sind, 0.0)

    def diff_qk(col0, gain_ref, out_ref):
        y = _dot(h, win_ref[:, col0:col0 + DIFF_HEADS * LANES])
        gain = gain_ref[...]
        for hd in range(DIFF_HEADS):
            yh = y[:, hd * LANES:(hd + 1) * LANES]
            sq = yh * yh
            s_all = jnp.sum(sq, axis=-1, keepdims=True)
            s0 = jnp.sum(jnp.where(is_map0, sq, 0.0), axis=-1, keepdims=True)
            r0 = lax.rsqrt(s0 * (1.0 / DIFF_HEAD_DIM) + EPS)
            r1 = lax.rsqrt((s_all - s0) * (1.0 / DIFF_HEAD_DIM) + EPS)
            yn = yh * jnp.where(is_map0, r0, r1) * gain
            out_ref[rs, hd * LANES:(hd + 1) * LANES] = _rope64(yn, cosd, sind).astype(BF16)

    def diff_v():
        dv_ref[rs, :] = _dot(h, win_ref[:, C_DV:C_DV + DIFF_HEADS * LANES]).astype(BF16)

    def gate(br):
        sl = slice(br * D_MODEL, (br + 1) * D_MODEL)
        z = _dot(h, win_ref[:, C_GATE + br * D_MODEL:C_GATE + (br + 1) * D_MODEL]) + bgate_ref[:, sl]
        gate_ref[rs, sl] = (1.0 / (1.0 + jnp.exp(-z))).astype(BF16)

    diff_qk(C_DQ, dqg_ref, dq_ref)
    diff_qk(C_DK, dkg_ref, dk_ref)
    diff_v()

    lat = _dot(h, win_ref[:, C_CQ:C_XQ])
    cq = lat[:, 0:MLA_Q_LORA]
    ckv = lat[:, MLA_Q_LORA:MLA_Q_LORA + MLA_KV_LORA]
    kr = lat[:, MLA_Q_LORA + MLA_KV_LORA:]
    cqn = (cq * _inv_rms(cq, MLA_Q_LORA) * qag_ref[...]).astype(BF16)
    ckvn = (ckv * _inv_rms(ckv, MLA_KV_LORA) * kvag_ref[...]).astype(BF16)
    kr = _rope64(kr, cosm, sinm)

    q = _dot(cqn, wqb_ref[...])
    kn = _dot(ckvn, wkvk_ref[...])
    mv_ref[rs, :] = _dot(ckvn, wkvv_ref[...]).astype(BF16)
    mqg, mkg = mqg_ref[...], mkg_ref[...]
    for hd in range(MLA_HEADS):
        sl = slice(hd * LANES, (hd + 1) * LANES)
        qh = _rope64(q[:, sl], cosm, sinm)
        mq_ref[rs, sl] = (qh * _inv_rms(qh, MLA_QK_DIM) * mqg).astype(BF16)
        kh = kn[:, sl] + kr
        mk_ref[rs, sl] = (kh * _inv_rms(kh, MLA_QK_DIM) * mkg).astype(BF16)

    xq = _dot(h, win_ref[:, C_XQ:C_GATE])
    xqg = xqg_ref[...]
    for hd in range(CROSS_HEADS):
        sl = slice(hd * LANES, (hd + 1) * LANES)
        yh = xq[:, sl]
        xq_ref[rs, sl] = (yh * _inv_rms(yh, CROSS_HEAD_DIM) * xqg).astype(BF16)
    for br in range(3):
        gate(br)


def _inproj(x, tabs, p):
    t = x.shape[0]
    tm = min(TM_IN, t)
    row = lambda w: pl.BlockSpec((tm, w), lambda i: (i, 0))
    consts = [p["g_mix"], p["w_in"], p["b_gate"], p["dq_gain"], p["dk_gain"],
              p["qa_gain"], p["w_qb"], p["kva_gain"], p["w_kvk"], p["w_kvv"],
              p["mq_gain"], p["mk_gain"], p["xq_gain"]]
    out_widths = [DIFF_HEADS * LANES] * 3 + [MLA_HEADS * LANES] * 2 + [MLA_HEADS * MLA_V_DIM,
                                                                      CROSS_HEADS * CROSS_HEAD_DIM, 3 * D_MODEL]
    return pl.pallas_call(
        _inproj_kernel,
        grid=(t // tm,),
        in_specs=[row(D_MODEL)] + [row(LANES)] * 2 + [_const_spec(c.shape) for c in consts],
        out_specs=[row(w) for w in out_widths],
        out_shape=[jax.ShapeDtypeStruct((t, w), BF16) for w in out_widths],
        compiler_params=pltpu.CompilerParams(dimension_semantics=("parallel",),
                                             vmem_limit_bytes=VMEM_LIMIT),
        name="inproj",
    )(x, *tabs, *consts)


def _causal_bias_t(tq):
    r = lax.broadcasted_iota(jnp.int32, (tq, tq), 0)
    c = lax.broadcasted_iota(jnp.int32, (tq, tq), 1)
    return jnp.where(r <= c, 0.0, -jnp.inf).astype(F32)


def _fold_rows(a, op):
    n, w = a.shape
    return op(op(a.reshape(n // SUBLANES, SUBLANES, w), axis=0), axis=0, keepdims=True)


def _causal_attention(jobs, tq, k_ref, s_ref, p_ref):
    bias_t = _causal_bias_t(tq)
    n_jobs = len(jobs)

    def scores(j):
        i, make_kq, _, _ = jobs[j]
        lo, hi = i * tq, (i + 1) * tq
        s_buf = s_ref.at[j % 2]
        maxima = []
        for k_sl, q_rows, s_sl in make_kq():
            reps = q_rows.shape[0] // tq
            bias = bias_t if reps == 1 else jnp.concatenate([bias_t] * reps, axis=1)
            s = _dot_nt(k_ref[lo:hi, k_sl], q_rows) + bias
            s_buf[lo:hi, s_sl] = s
            m = _fold_rows(s, jnp.max)
            if lo > 0:
                s = _dot_nt(k_ref[0:lo, k_sl], q_rows)
                s_buf[0:lo, s_sl] = s
                m = jnp.maximum(m, _fold_rows(s, jnp.max))
            maxima.append(m)
        return maxima[0] if len(maxima) == 1 else jnp.concatenate(maxima, axis=1)

    def softmax(j, m):
        hi = (jobs[j][0] + 1) * tq
        s_buf, p_buf = s_ref.at[j % 2], p_ref.at[j % 2]
        for r in range(0, hi, SM_ROWS):
            p_buf[r:r + SM_ROWS, :] = jnp.exp2(s_buf[r:r + SM_ROWS, :] - m).astype(BF16)

    def pv(j):
        i, _, vt_ref, finish = jobs[j]
        hi = (i + 1) * tq
        acc = _dot(vt_ref[:, 0:hi], p_ref[j % 2, 0:hi, :])
        finish(acc[0:LANES], acc[LANES:LANES + 1])

    m_next = scores(0)
    for j in range(n_jobs):
        m_cur = m_next
        if j + 1 < n_jobs:
            m_next = scores(j + 1)
        softmax(j, m_cur)
        if j > 0:
            pv(j - 1)
    pv(n_jobs - 1)


VT_ROWS = LANES + 16


def _store_vt(v_ref, vt_ref):
    for g in range(vt_ref.shape[0]):
        vt_ref[g, 0:LANES, :] = v_ref[:, g * LANES:(g + 1) * LANES].astype(F32).T.astype(BF16)
        vt_ref[g, LANES:VT_ROWS, :] = jnp.ones((VT_ROWS - LANES, vt_ref.shape[2]), BF16)


def _diff_attn_kernel(lam_ref, og_ref, q_ref, k_ref, v_ref, o_ref, vt_ref, s_ref, p_ref):
    s_len = q_ref.shape[0]
    tq = TQ
    lane = lax.broadcasted_iota(jnp.int32, (tq, LANES), 1)
    is_map0 = (lane & 32) == 0
    _store_vt(v_ref, vt_ref)

    lv = lam_ref[...]
    lam = (jnp.exp(jnp.sum(lv[0:1] * lv[1:2], axis=-1, keepdims=True))
           - jnp.exp(jnp.sum(lv[2:3] * lv[3:4], axis=-1, keepdims=True))
           + lv[4:5, 0:1])
    og = og_ref[...] * (1.0 - lv[4:5, 0:1])
    og = jnp.concatenate([og] * (tq // LANES), axis=1)

    def job(g, i):
        cols = slice(g * LANES, (g + 1) * LANES)
        rows = slice(i * tq, (i + 1) * tq)

        def make_kq():
            q = q_ref[rows, cols]
            zero = jnp.zeros_like(q)
            qq = jnp.concatenate([jnp.where(is_map0, q, zero), jnp.where(is_map0, zero, q)], axis=0)
            return [(cols, qq, slice(None))]

        def finish(acc, l):
            o = acc / l
            o = o[:, 0:tq] - lam * o[:, tq:2 * tq]
            r = lax.rsqrt(jnp.sum(o * o, axis=0, keepdims=True) * (1.0 / (2 * DIFF_HEAD_DIM)) + EPS)
            o_ref[rows, cols] = (o * r * og).T.astype(BF16)

        return i, make_kq, vt_ref.at[g], finish

    jobs = [job(g, i) for g in range(vt_ref.shape[0]) for i in range(s_len // tq)]
    _causal_attention(jobs, tq, k_ref, s_ref, p_ref)


def _attn_scratch(seq):
    return [pltpu.VMEM((ATTN_GROUPS, VT_ROWS, seq), BF16),
            pltpu.VMEM((2, seq, 2 * TQ), F32),
            pltpu.VMEM((2, seq, 2 * TQ), BF16)]


def _diff_attn(lam_tab, out_gain_rows, dq, dk, dv, batch, seq):
    blk = pl.BlockSpec((None, seq, ATTN_GROUPS * LANES), lambda b, h: (b, 0, h))
    shp = (batch, seq, DIFF_HEADS * LANES)
    out = pl.pallas_call(
        _diff_attn_kernel,
        grid=(batch, DIFF_HEADS // ATTN_GROUPS),
        in_specs=[_const_spec(lam_tab.shape), _const_spec(out_gain_rows.shape), blk, blk, blk],
        out_specs=blk,
        out_shape=jax.ShapeDtypeStruct(shp, BF16),
        scratch_shapes=_attn_scratch(seq),
        compiler_params=pltpu.CompilerParams(dimension_semantics=("parallel", "parallel"),
                                             vmem_limit_bytes=VMEM_LIMIT),
        name="diff_attn",
    )(lam_tab, out_gain_rows, dq.reshape(shp), dk.reshape(shp), dv.reshape(shp))
    return out.reshape(batch * seq, DIFF_HEADS * LANES)


def _mla_attn_kernel(q_ref, k_ref, v_ref, o_ref, vt_ref, s_ref, p_ref):
    s_len = q_ref.shape[0]
    tq = TQ
    _store_vt(v_ref, vt_ref)
    row = lax.broadcasted_iota(jnp.int32, (LANES, tq), 0)
    is_head0 = row < MLA_V_DIM

    def job(g, i):
        rows = slice(i * tq, (i + 1) * tq)

        def make_kq():
            heads = [slice((2 * g + hd) * LANES, (2 * g + hd + 1) * LANES) for hd in range(2)]
            return [(heads[hd], q_ref[rows, heads[hd]], slice(hd * tq, (hd + 1) * tq)) for hd in range(2)]

        def finish(acc, l):
            o = acc / l
            o = jnp.where(is_head0, o[:, 0:tq], o[:, tq:2 * tq])
            o_ref[rows, g * LANES:(g + 1) * LANES] = o.T.astype(BF16)

        return i, make_kq, vt_ref.at[g], finish

    jobs = [job(g, i) for g in range(vt_ref.shape[0]) for i in range(s_len // tq)]
    _causal_attention(jobs, tq, k_ref, s_ref, p_ref)


def _mla_attn(mq, mk, mv, batch, seq):
    pairs = ATTN_GROUPS
    qk_blk = pl.BlockSpec((None, seq, pairs * 2 * LANES), lambda b, h: (b, 0, h))
    v_blk = pl.BlockSpec((None, seq, pairs * 2 * MLA_V_DIM), lambda b, h: (b, 0, h))
    qk_shp = (batch, seq, MLA_HEADS * LANES)
    v_shp = (batch, seq, MLA_HEADS * MLA_V_DIM)
    out = pl.pallas_call(
        _mla_attn_kernel,
        grid=(batch, MLA_HEADS // (2 * pairs)),
        in_specs=[qk_blk, qk_blk, v_blk],
        out_specs=v_blk,
        out_shape=jax.ShapeDtypeStruct(v_shp, BF16),
        scratch_shapes=_attn_scratch(seq),
        compiler_params=pltpu.CompilerParams(dimension_semantics=("parallel", "parallel"),
                                             vmem_limit_bytes=VMEM_LIMIT),
        name="mla_attn",
    )(mq.reshape(qk_shp), mk.reshape(qk_shp), mv.reshape(v_shp))
    return out.reshape(batch * seq, MLA_HEADS * MLA_V_DIM)


def _mem_kv_kernel(mem_ref, mg_ref, w_ref, kg_ref, k_ref, v_ref):
    m = mem_ref[...]
    mn = (m * _inv_rms(m, D_MODEL) * mg_ref[...]).astype(BF16)
    kv = _dot(mn, w_ref[...])
    width = CROSS_HEADS * CROSS_HEAD_DIM
    kg = kg_ref[...]
    for hd in range(CROSS_HEADS):
        sl = slice(hd * LANES, (hd + 1) * LANES)
        kh = kv[:, sl]
        k_ref[:, sl] = (kh * _inv_rms(kh, CROSS_HEAD_DIM) * kg).astype(BF16)
    v_ref[...] = kv[:, width:].astype(BF16)


def _mem_kv(mem2d, p):
    rows = mem2d.shape[0]
    width = CROSS_HEADS * CROSS_HEAD_DIM
    tm = min(512, rows)
    return pl.pallas_call(
        _mem_kv_kernel,
        grid=(rows // tm,),
        in_specs=[pl.BlockSpec((tm, D_MODEL), lambda i: (i, 0)), _const_spec(p["mem_gain"].shape),
                  _const_spec(p["w_mem_kv"].shape), _const_spec(p["ck_gain"].shape)],
        out_specs=[pl.BlockSpec((tm, width), lambda i: (i, 0))] * 2,
        out_shape=[jax.ShapeDtypeStruct((rows, width), BF16)] * 2,
        compiler_params=pltpu.CompilerParams(dimension_semantics=("parallel",),
                                             vmem_limit_bytes=VMEM_LIMIT),
        name="mem_kv",
    )(mem2d, p["mem_gain"], p["w_mem_kv"], p["ck_gain"])


FF_CHUNK = 1024


def _post_kernel(x_ref, od_ref, om_ref, xq_ref, ck_ref, cv_ref, g_ref,
                 wdo_ref, wmo_ref, wco_ref, wout_ref, gmlp_ref, w1_ref, w2_ref, o_ref):
    heads = []
    for hd in range(CROSS_HEADS):
        sl = slice(hd * LANES, (hd + 1) * LANES)
        s = _dot_nt(xq_ref[:, sl], ck_ref[:, sl])
        p = jnp.exp2(s - jnp.max(s, axis=-1, keepdims=True))
        l = jnp.sum(p, axis=-1, keepdims=True)
        heads.append((_dot(p.astype(BF16), cv_ref[:, sl]) / l).astype(BF16))
    oc = jnp.concatenate(heads, axis=1)

    merged = (g_ref[:, 0:D_MODEL].astype(F32) * _dot(od_ref[...], wdo_ref[...])
              + g_ref[:, D_MODEL:2 * D_MODEL].astype(F32) * _dot(om_ref[...], wmo_ref[...])
              + g_ref[:, 2 * D_MODEL:].astype(F32) * _dot(oc, wco_ref[...]))
    x = x_ref[...] + _dot(merged.astype(BF16), wout_ref[...])

    h = (x * _inv_rms(x, D_MODEL) * gmlp_ref[...]).astype(BF16)
    y = x
    for c in range(D_FF // FF_CHUNK):
        sl = slice(c * FF_CHUNK, (c + 1) * FF_CHUNK)
        u = jnp.maximum(_dot(h, w1_ref[:, sl]), 0.0)
        y = y + _dot((u * u).astype(BF16), w2_ref[sl, :])
    o_ref[...] = y


def _post(x, od, om, xq, ck, cv, gates, p, seq):
    t = x.shape[0]
    tm = min(TM_OUT, seq)
    tiles_per_batch = seq // tm
    width = CROSS_HEADS * CROSS_HEAD_DIM
    row = lambda w: pl.BlockSpec((tm, w), lambda i: (i, 0))
    mem_blk = pl.BlockSpec((MEM_LEN, width), lambda i: (i // tiles_per_batch, 0))
    consts = [p["w_diff_o"], p["w_mla_o"], p["w_cross_o"], p["w_out"], p["g_mlp"], p["w_mlp1"], p["w_mlp2"]]
    return pl.pallas_call(
        _post_kernel,
        grid=(t // tm,),
        in_specs=[row(D_MODEL), row(od.shape[1]), row(om.shape[1]), row(width), mem_blk, mem_blk,
                  row(gates.shape[1])] + [_const_spec(c.shape) for c in consts],
        out_specs=row(D_MODEL),
        out_shape=jax.ShapeDtypeStruct((t, D_MODEL), F32),
        compiler_params=pltpu.CompilerParams(dimension_semantics=("parallel",),
                                             vmem_limit_bytes=VMEM_LIMIT),
        name="post",
    )(x, od, om, xq, ck, cv, gates, *consts)


def _pack_params(w_in, b_gate, g_mix, diff_q_gain, diff_k_gain, diff_lambda, diff_out_gain, w_diff_o,
                 mla_qa_gain, w_mla_qb, mla_kva_gain, w_mla_kvb, mla_q_gain, mla_k_gain, w_mla_o,
                 mem_gain, w_mem_kv, cross_q_gain, cross_k_gain, w_cross_o, w_out, g_mlp, w_mlp1, w_mlp2):
    nl = w_in.shape[0]
    d = D_MODEL
    half = DIFF_HEAD_DIM // 2
    o_dq, o_dk, o_dv = 0, 1024, 2048
    o_cq = 3072
    o_ckv = o_cq + MLA_Q_LORA
    o_kr = o_ckv + MLA_KV_LORA
    o_xq = o_kr + MLA_ROPE_DIM

    def diff_cols(w):
        r = w.shape[1]
        return w.reshape(nl, r, DIFF_HEADS, 2, 2, half // 2, 2).transpose(0, 1, 2, 4, 3, 6, 5).reshape(
            nl, r, DIFF_HEADS * LANES)

    def diff_gain(g, scale):
        g = g.reshape(nl, 2, 1, half // 2, 2).transpose(0, 1, 2, 4, 3)
        g = jnp.broadcast_to(g, (nl, 2, 2, 2, half // 2)).reshape(nl, 1, LANES)
        return (g * scale).astype(F32)

    rh = MLA_ROPE_DIM // 2

    def mla_head_layout(nope, rope):
        pad = jnp.zeros(nope.shape[:-1] + (LANES - MLA_QK_DIM,), nope.dtype)
        return jnp.concatenate([rope[..., :rh], nope[..., :HALF_LANES - rh], rope[..., rh:],
                                nope[..., HALF_LANES - rh:], pad], axis=-1)

    kr_w = w_in[:, :, o_kr:o_xq]
    zpad = jnp.zeros((nl, d, HALF_LANES - rh), F32)
    kr_cols = jnp.concatenate([kr_w[..., :rh], zpad, kr_w[..., rh:], zpad], axis=-1)
    w_in_p = jnp.concatenate([diff_cols(w_in[:, :, o_dq:o_dk]).astype(BF16),
                              diff_cols(w_in[:, :, o_dk:o_dv]).astype(BF16),
                              w_in[:, :, o_dv:o_kr].astype(BF16), kr_cols.astype(BF16),
                              w_in[:, :, o_xq:].astype(BF16)], axis=-1)

    wqb = w_mla_qb.reshape(nl, MLA_Q_LORA, MLA_HEADS, MLA_QK_DIM)
    w_qb_p = mla_head_layout(wqb[..., :MLA_NOPE_DIM], wqb[..., MLA_NOPE_DIM:]).reshape(
        nl, MLA_Q_LORA, MLA_HEADS * LANES).astype(BF16)
    wkvb = w_mla_kvb.reshape(nl, MLA_KV_LORA, MLA_HEADS, MLA_NOPE_DIM + MLA_V_DIM)
    k_nope = wkvb[..., :MLA_NOPE_DIM]
    w_kvk_p = mla_head_layout(k_nope, jnp.zeros(k_nope.shape[:-1] + (MLA_ROPE_DIM,), F32)).reshape(
        nl, MLA_KV_LORA, MLA_HEADS * LANES).astype(BF16)
    w_kvv_p = wkvb[..., MLA_NOPE_DIM:].reshape(nl, MLA_KV_LORA, MLA_HEADS * MLA_V_DIM).astype(BF16)

    def mla_gain(g, scale):
        return (mla_head_layout(g[:, :MLA_NOPE_DIM], g[:, MLA_NOPE_DIM:]) * scale)[:, None, :].astype(F32)

    lam_init = jnp.asarray([0.8 - 0.6 * math.exp(-0.3 * l) for l in range(nl)], F32)
    lam_tab = jnp.zeros((nl, SUBLANES, LANES), F32)
    lam_tab = lam_tab.at[:, 0:4, 0:DIFF_HEAD_DIM].set(diff_lambda.astype(F32))
    lam_tab = lam_tab.at[:, 4, :].set(lam_init[:, None])
    out_gain_rows = jnp.broadcast_to(diff_out_gain[:, :, None], (nl, 2 * DIFF_HEAD_DIM, LANES)).astype(F32)

    row = lambda g: g[:, None, :]
    return dict(
        g_mix=row(g_mix), w_in=w_in_p, b_gate=row(b_gate),
        dq_gain=diff_gain(diff_q_gain, DIFF_HEAD_DIM ** -0.5 * LOG2E), dk_gain=diff_gain(diff_k_gain, 1.0),
        qa_gain=row(mla_qa_gain), w_qb=w_qb_p, kva_gain=row(mla_kva_gain),
        w_kvk=w_kvk_p, w_kvv=w_kvv_p,
        mq_gain=mla_gain(mla_q_gain, MLA_QK_DIM ** -0.5 * LOG2E), mk_gain=mla_gain(mla_k_gain, 1.0),
        xq_gain=row(cross_q_gain * (CROSS_HEAD_DIM ** -0.5 * LOG2E)),
        lam_tab=lam_tab, out_gain_rows=out_gain_rows,
        w_diff_o=w_diff_o.astype(BF16), w_mla_o=w_mla_o.astype(BF16),
        mem_gain=row(mem_gain), w_mem_kv=w_mem_kv.astype(BF16), ck_gain=row(cross_k_gain),
        w_cross_o=w_cross_o.astype(BF16), w_out=w_out.astype(BF16),
        g_mlp=row(g_mlp), w_mlp1=w_mlp1.astype(BF16), w_mlp2=w_mlp2.astype(BF16),
    )


def kernel(x, mem, positions, g_mix, w_in, b_gate, diff_q_gain, diff_k_gain, diff_lambda, diff_out_gain,
           w_diff_o, mla_qa_gain, w_mla_qb, mla_kva_gain, w_mla_kvb, mla_q_gain, mla_k_gain, w_mla_o,
           mem_gain, w_mem_kv, cross_q_gain, cross_k_gain, w_cross_o, w_out, g_mlp, w_mlp1, w_mlp2):
    batch, seq, d = x.shape
    tabs = _rope_tables(positions)
    packed = _pack_params(w_in, b_gate, g_mix, diff_q_gain, diff_k_gain, diff_lambda, diff_out_gain,
                          w_diff_o, mla_qa_gain, w_mla_qb, mla_kva_gain, w_mla_kvb, mla_q_gain, mla_k_gain,
                          w_mla_o, mem_gain, w_mem_kv, cross_q_gain, cross_k_gain, w_cross_o, w_out,
                          g_mlp, w_mlp1, w_mlp2)
    xt = x.reshape(batch * seq, d)
    mem2d = mem.reshape(batch * mem.shape[1], d)
    for l in range(w_in.shape[0]):
        p = {name: a[l] for name, a in packed.items()}
        dq, dk, dv, mq, mk, mv, xq, gates = _inproj(xt, tabs, p)
        od = _diff_attn(p["lam_tab"], p["out_gain_rows"], dq, dk, dv, batch, seq)
        om = _mla_attn(mq, mk, mv, batch, seq)
        ck, cv = _mem_kv(mem2d, p)
        xt = _post(xt, od, om, xq, ck, cv, gates, p, seq)
    return xt.reshape(batch, seq, d)
```

```python
import math

import jax
import jax.numpy as jnp
from jax import lax
from jax.experimental import pallas as pl
from jax.experimental.pallas import tpu as pltpu

D_MODEL = 1024
DEPTH = 4
MEM_LEN = 256
ROPE_THETA = 10000.0
EPS = 1e-6

DIFF_HEADS = 8
DIFF_HEAD_DIM = 64
MLA_HEADS = 8
MLA_Q_LORA = 384
MLA_KV_LORA = 256
MLA_NOPE_DIM = 64
MLA_ROPE_DIM = 32
MLA_V_DIM = 64
MLA_QK_DIM = MLA_NOPE_DIM + MLA_ROPE_DIM
CROSS_HEADS = 4
CROSS_HEAD_DIM = 128
D_FF = 4 * D_MODEL

LANES = 128
SUBLANES = 8
HALF_LANES = LANES // 2
LOG2E = math.log2(math.e)

C_DQ = 0
C_DK = C_DQ + DIFF_HEADS * LANES
C_DV = C_DK + DIFF_HEADS * LANES
C_CQ = C_DV + DIFF_HEADS * LANES
C_CKV = C_CQ + MLA_Q_LORA
C_KR = C_CKV + MLA_KV_LORA
C_XQ = C_KR + LANES
C_GATE = C_XQ + CROSS_HEADS * CROSS_HEAD_DIM
IN_WIDTH_PACKED = C_GATE + 3 * D_MODEL

TM_IN = 512
TM_OUT = 512
TQ = 256
ATTN_GROUPS = 2
SM_ROWS = 32
VMEM_LIMIT = 56 * 1024 * 1024

BF16 = jnp.bfloat16
F32 = jnp.float32


def _const_spec(shape):
    nd = len(shape)
    return pl.BlockSpec(shape, lambda *_: (0,) * nd, pipeline_mode=pl.Buffered(1))


def _dot(a, b):
    return jnp.dot(a, b, preferred_element_type=F32)


def _dot_nt(a, b):
    return lax.dot_general(a, b, (((1,), (1,)), ((), ())), preferred_element_type=F32)


def _inv_rms(y, n):
    return lax.rsqrt(jnp.sum(y * y, axis=-1, keepdims=True) * (1.0 / n) + EPS)


def _rope64(y, cos, sin):
    return y * cos + pltpu.roll(y, HALF_LANES, axis=1) * sin


ROPE_FREQS = DIFF_HEAD_DIM // 2
ROPE_GROUPS = LANES // ROPE_FREQS


def _rope_table_kernel(pos_ref, invf_ref, cos_ref, sin_ref):
    ang = pos_ref[...].astype(F32) * invf_ref[...]
    cos, sin = jnp.cos(ang), jnp.sin(ang)
    lane = lax.broadcasted_iota(jnp.int32, ang.shape, 1)
    sign = jnp.where(lane < HALF_LANES, -1.0, 1.0)

    def spread(x, k):
        if k:
            x = pltpu.roll(x, LANES - ROPE_FREQS * k, axis=1)
        x = jnp.where(lane < ROPE_FREQS, x, pltpu.roll(x, ROPE_FREQS, axis=1))
        return jnp.where(lane < HALF_LANES, x, pltpu.roll(x, HALF_LANES, axis=1))

    for k in range(ROPE_GROUPS):
        cos_ref[k] = spread(cos, k)
        sin_ref[k] = spread(sin, k) * sign


def _rope_tables(positions):
    t = positions.size
    rows = t // ROPE_GROUPS
    inv_freq = ROPE_THETA ** (-jnp.arange(0, DIFF_HEAD_DIM, 2, dtype=F32) / DIFF_HEAD_DIM)
    inv_freq = jnp.concatenate([inv_freq[0::2], inv_freq[1::2]])
    pos_rep = jnp.repeat(positions.reshape(ROPE_GROUPS, rows).T, ROPE_FREQS, axis=1)
    invf = jnp.tile(inv_freq, ROPE_GROUPS)[None, :]
    tr = min(1024, rows)
    out_blk = pl.BlockSpec((ROPE_GROUPS, tr, LANES), lambda i: (0, i, 0))
    cos, sin = pl.pallas_call(
        _rope_table_kernel,
        grid=(rows // tr,),
        in_specs=[pl.BlockSpec((tr, LANES), lambda i: (i, 0)),
                  pl.BlockSpec((1, LANES), lambda i: (0, 0))],
        out_specs=[out_blk] * 2,
        out_shape=[jax.ShapeDtypeStruct((ROPE_GROUPS, rows, LANES), F32)] * 2,
        name="rope_tables",
    )(pos_rep, invf)
    return cos.reshape(t, LANES), sin.reshape(t, LANES)


def _inproj_kernel(x_ref, cosd_ref, sind_ref,
                   gmix_ref, win_ref, bgate_ref, dqg_ref, dkg_ref,
                   qag_ref, wqb_ref, kvag_ref, wkvk_ref, wkvv_ref, mqg_ref, mkg_ref, xqg_ref,
                   dq_ref, dk_ref, dv_ref, mq_ref, mk_ref, mv_ref, xq_ref, gate_ref):
    x = x_ref[...]
    h = (x * _inv_rms(x, D_MODEL) * gmix_ref[...]).astype(BF16)
    tm = x.shape[0]
    rs = slice(None)

    cosd, sind = cosd_ref[...], sind_ref[...]
    lane = lax.broadcasted_iota(jnp.int32, (tm, LANES), 1)
    is_map0 = (lane & 32) == 0
    is_mla_rope = (lane & (HALF_LANES - MLA_ROPE_DIM // 2)) == 0
    cosm = jnp.where(is_mla_rope, cosd, 1.0)
    sinm = jnp.where(is_mla_rope, sind, 0.0)

    def diff_qk(col0, gain_ref, out_ref):
        y = _dot(h, win_ref[:, col0:col0 + DIFF_HEADS * LANES])
        gain = gain_ref[...]
        for hd in range(DIFF_HEADS):
            yh = y[:, hd * LANES:(hd + 1) * LANES]
            sq = yh * yh
            s_all = jnp.sum(sq, axis=-1, keepdims=True)
            s0 = jnp.sum(jnp.where(is_map0, sq, 0.0), axis=-1, keepdims=True)
            r0 = lax.rsqrt(s0 * (1.0 / DIFF_HEAD_DIM) + EPS)
            r1 = lax.rsqrt((s_all - s0) * (1.0 / DIFF_HEAD_DIM) + EPS)
            yn = yh * jnp.where(is_map0, r0, r1) * gain
            out_ref[rs, hd * LANES:(hd + 1) * LANES] = _rope64(yn, cosd, sind).astype(BF16)

    def diff_v():
        dv_ref[rs, :] = _dot(h, win_ref[:, C_DV:C_DV + DIFF_HEADS * LANES]).astype(BF16)

    def gate(br):
        sl = slice(br * D_MODEL, (br + 1) * D_MODEL)
        z = _dot(h, win_ref[:, C_GATE + br * D_MODEL:C_GATE + (br + 1) * D_MODEL]) + bgate_ref[:, sl]
        gate_ref[rs, sl] = (1.0 / (1.0 + jnp.exp(-z))).astype(BF16)

    diff_qk(C_DQ, dqg_ref, dq_ref)
    diff_qk(C_DK, dkg_ref, dk_ref)
    diff_v()

    lat = _dot(h, win_ref[:, C_CQ:C_XQ])
    cq = lat[:, 0:MLA_Q_LORA]
    ckv = lat[:, MLA_Q_LORA:MLA_Q_LORA + MLA_KV_LORA]
    kr = lat[:, MLA_Q_LORA + MLA_KV_LORA:]
    cqn = (cq * _inv_rms(cq, MLA_Q_LORA) * qag_ref[...]).astype(BF16)
    ckvn = (ckv * _inv_rms(ckv, MLA_KV_LORA) * kvag_ref[...]).astype(BF16)
    kr = _rope64(kr, cosm, sinm)

    q = _dot(cqn, wqb_ref[...])
    kn = _dot(ckvn, wkvk_ref[...])
    mv_ref[rs, :] = _dot(ckvn, wkvv_ref[...]).astype(BF16)
    mqg, mkg = mqg_ref[...], mkg_ref[...]
    for hd in range(MLA_HEADS):
        sl = slice(hd * LANES, (hd + 1) * LANES)
        qh = _rope64(q[:, sl], cosm, sinm)
        mq_ref[rs, sl] = (qh * _inv_rms(qh, MLA_QK_DIM) * mqg).astype(BF16)
        kh = kn[:, sl] + kr
        mk_ref[rs, sl] = (kh * _inv_rms(kh, MLA_QK_DIM) * mkg).astype(BF16)

    xq = _dot(h, win_ref[:, C_XQ:C_GATE])
    xqg = xqg_ref[...]
    for hd in range(CROSS_HEADS):
        sl = slice(hd * LANES, (hd + 1) * LANES)
        yh = xq[:, sl]
        xq_ref[rs, sl] = (yh * _inv_rms(yh, CROSS_HEAD_DIM) * xqg).astype(BF16)
    for br in range(3):
        gate(br)


def _inproj(x, tabs, p):
    t = x.shape[0]
    tm = min(TM_IN, t)
    row = lambda w: pl.BlockSpec((tm, w), lambda i: (i, 0))
    consts = [p["g_mix"], p["w_in"], p["b_gate"], p["dq_gain"], p["dk_gain"],
              p["qa_gain"], p["w_qb"], p["kva_gain"], p["w_kvk"], p["w_kvv"],
              p["mq_gain"], p["mk_gain"], p["xq_gain"]]
    out_widths = [DIFF_HEADS * LANES] * 3 + [MLA_HEADS * LANES] * 2 + [MLA_HEADS * MLA_V_DIM,
                                                                      CROSS_HEADS * CROSS_HEAD_DIM, 3 * D_MODEL]
    return pl.pallas_call(
        _inproj_kernel,
        grid=(t // tm,),
        in_specs=[row(D_MODEL)] + [row(LANES)] * 2 + [_const_spec(c.shape) for c in consts],
        out_specs=[row(w) for w in out_widths],
        out_shape=[jax.ShapeDtypeStruct((t, w), BF16) for w in out_widths],
        compiler_params=pltpu.CompilerParams(dimension_semantics=("parallel",),
                                             vmem_limit_bytes=VMEM_LIMIT),
        name="inproj",
    )(x, *tabs, *consts)


def _causal_bias_t(tq):
    r = lax.broadcasted_iota(jnp.int32, (tq, tq), 0)
    c = lax.broadcasted_iota(jnp.int32, (tq, tq), 1)
    return jnp.where(r <= c, 0.0, -jnp.inf).astype(F32)


def _fold_rows(a, op):
    n, w = a.shape
    return op(op(a.reshape(n // SUBLANES, SUBLANES, w), axis=0), axis=0, keepdims=True)


def _causal_attention(jobs, tq, k_ref, s_ref, p_ref):
    bias_t = _causal_bias_t(tq)
    n_jobs = len(jobs)

    def scores(j):
        i, make_kq, _, _ = jobs[j]
        lo, hi = i * tq, (i + 1) * tq
        s_buf = s_ref.at[j % 2]
        maxima = []
        for k_sl, q_rows, s_sl in make_kq():
            reps = q_rows.shape[0] // tq
            bias = bias_t if reps == 1 else jnp.concatenate([bias_t] * reps, axis=1)
            s = _dot_nt(k_ref[lo:hi, k_sl], q_rows) + bias
            s_buf[lo:hi, s_sl] = s
            m = _fold_rows(s, jnp.max)
            if lo > 0:
                s = _dot_nt(k_ref[0:lo, k_sl], q_rows)
                s_buf[0:lo, s_sl] = s
                m = jnp.maximum(m, _fold_rows(s, jnp.max))
            maxima.append(m)
        return maxima[0] if len(maxima) == 1 else jnp.concatenate(maxima, axis=1)

    def softmax(j, m):
        hi = (jobs[j][0] + 1) * tq
        s_buf, p_buf = s_ref.at[j % 2], p_ref.at[j % 2]
        for r in range(0, hi, SM_ROWS):
            p_buf[r:r + SM_ROWS, :] = jnp.exp2(s_buf[r:r + SM_ROWS, :] - m).astype(BF16)

    def pv(j):
        i, _, vt_ref, finish = jobs[j]
        hi = (i + 1) * tq
        acc = _dot(vt_ref[:, 0:hi], p_ref[j % 2, 0:hi, :])
        finish(acc[0:LANES], acc[LANES:LANES + 1])

    m_next = scores(0)
    for j in range(n_jobs):
        m_cur = m_next
        if j + 1 < n_jobs:
            m_next = scores(j + 1)
        softmax(j, m_cur)
        if j > 0:
            pv(j - 1)
    pv(n_jobs - 1)


def _job_order(job, n_groups, n_tiles):
    return [job(g, i) for i in range(n_tiles) for g in range(n_groups)]


VT_ROWS = LANES + 16


def _store_vt(v_ref, vt_ref):
    for g in range(vt_ref.shape[0]):
        vt_ref[g, 0:LANES, :] = v_ref[:, g * LANES:(g + 1) * LANES].astype(F32).T.astype(BF16)
        vt_ref[g, LANES:VT_ROWS, :] = jnp.ones((VT_ROWS - LANES, vt_ref.shape[2]), BF16)


def _diff_attn_kernel(lam_ref, og_ref, q_ref, k_ref, v_ref, o_ref, vt_ref, s_ref, p_ref):
    s_len = q_ref.shape[0]
    tq = TQ
    lane = lax.broadcasted_iota(jnp.int32, (tq, LANES), 1)
    is_map0 = (lane & 32) == 0
    _store_vt(v_ref, vt_ref)

    lv = lam_ref[...]
    lam = (jnp.exp(jnp.sum(lv[0:1] * lv[1:2], axis=-1, keepdims=True))
           - jnp.exp(jnp.sum(lv[2:3] * lv[3:4], axis=-1, keepdims=True))
           + lv[4:5, 0:1])
    og = og_ref[...] * (1.0 - lv[4:5, 0:1])
    og = jnp.concatenate([og] * (tq // LANES), axis=1)

    def job(g, i):
        cols = slice(g * LANES, (g + 1) * LANES)
        rows = slice(i * tq, (i + 1) * tq)

        def make_kq():
            q = q_ref[rows, cols]
            zero = jnp.zeros_like(q)
            qq = jnp.concatenate([jnp.where(is_map0, q, zero), jnp.where(is_map0, zero, q)], axis=0)
            return [(cols, qq, slice(None))]

        def finish(acc, l):
            o = acc / l
            o = o[:, 0:tq] - lam * o[:, tq:2 * tq]
            r = lax.rsqrt(jnp.sum(o * o, axis=0, keepdims=True) * (1.0 / (2 * DIFF_HEAD_DIM)) + EPS)
            o_ref[rows, cols] = (o * r * og).T.astype(BF16)

        return i, make_kq, vt_ref.at[g], finish

    jobs = _job_order(job, vt_ref.shape[0], s_len // tq)
    _causal_attention(jobs, tq, k_ref, s_ref, p_ref)


def _attn_scratch(seq):
    return [pltpu.VMEM((ATTN_GROUPS, VT_ROWS, seq), BF16),
            pltpu.VMEM((2, seq, 2 * TQ), F32),
            pltpu.VMEM((2, seq, 2 * TQ), BF16)]


def _diff_attn(lam_tab, out_gain_rows, dq, dk, dv, batch, seq):
    blk = pl.BlockSpec((None, seq, ATTN_GROUPS * LANES), lambda b, h: (b, 0, h))
    shp = (batch, seq, DIFF_HEADS * LANES)
    out = pl.pallas_call(
        _diff_attn_kernel,
        grid=(batch, DIFF_HEADS // ATTN_GROUPS),
        in_specs=[_const_spec(lam_tab.shape), _const_spec(out_gain_rows.shape), blk, blk, blk],
        out_specs=blk,
        out_shape=jax.ShapeDtypeStruct(shp, BF16),
        scratch_shapes=_attn_scratch(seq),
        compiler_params=pltpu.CompilerParams(dimension_semantics=("parallel", "parallel"),
                                             vmem_limit_bytes=VMEM_LIMIT),
        name="diff_attn",
    )(lam_tab, out_gain_rows, dq.reshape(shp), dk.reshape(shp), dv.reshape(shp))
    return out.reshape(batch * seq, DIFF_HEADS * LANES)


def _mla_attn_kernel(q_ref, k_ref, v_ref, o_ref, vt_ref, s_ref, p_ref):
    s_len = q_ref.shape[0]
    tq = TQ
    _store_vt(v_ref, vt_ref)
    row = lax.broadcasted_iota(jnp.int32, (LANES, tq), 0)
    is_head0 = row < MLA_V_DIM

    def job(g, i):
        rows = slice(i * tq, (i + 1) * tq)

        def make_kq():
            heads = [slice((2 * g + hd) * LANES, (2 * g + hd + 1) * LANES) for hd in range(2)]
            return [(heads[hd], q_ref[rows, heads[hd]], slice(hd * tq, (hd + 1) * tq)) for hd in range(2)]

        def finish(acc, l):
            o = acc / l
            o = jnp.where(is_head0, o[:, 0:tq], o[:, tq:2 * tq])
            o_ref[rows, g * LANES:(g + 1) * LANES] = o.T.astype(BF16)

        return i, make_kq, vt_ref.at[g], finish

    jobs = _job_order(job, vt_ref.shape[0], s_len // tq)
    _causal_attention(jobs, tq, k_ref, s_ref, p_ref)


def _mla_attn(mq, mk, mv, batch, seq):
    pairs = ATTN_GROUPS
    qk_blk = pl.BlockSpec((None, seq, pairs * 2 * LANES), lambda b, h: (b, 0, h))
    v_blk = pl.BlockSpec((None, seq, pairs * 2 * MLA_V_DIM), lambda b, h: (b, 0, h))
    qk_shp = (batch, seq, MLA_HEADS * LANES)
    v_shp = (batch, seq, MLA_HEADS * MLA_V_DIM)
    out = pl.pallas_call(
        _mla_attn_kernel,
        grid=(batch, MLA_HEADS // (2 * pairs)),
        in_specs=[qk_blk, qk_blk, v_blk],
        out_specs=v_blk,
        out_shape=jax.ShapeDtypeStruct(v_shp, BF16),
        scratch_shapes=_attn_scratch(seq),
        compiler_params=pltpu.CompilerParams(dimension_semantics=("parallel", "parallel"),
                                             vmem_limit_bytes=VMEM_LIMIT),
        name="mla_attn",
    )(mq.reshape(qk_shp), mk.reshape(qk_shp), mv.reshape(v_shp))
    return out.reshape(batch * seq, MLA_HEADS * MLA_V_DIM)


def _mem_kv_kernel(mem_ref, mg_ref, w_ref, kg_ref, k_ref, v_ref):
    m = mem_ref[...]
    mn = (m * _inv_rms(m, D_MODEL) * mg_ref[...]).astype(BF16)
    kv = _dot(mn, w_ref[...])
    width = CROSS_HEADS * CROSS_HEAD_DIM
    kg = kg_ref[...]
    for hd in range(CROSS_HEADS):
        sl = slice(hd * LANES, (hd + 1) * LANES)
        kh = kv[:, sl]
        k_ref[:, sl] = (kh * _inv_rms(kh, CROSS_HEAD_DIM) * kg).astype(BF16)
    v_ref[...] = kv[:, width:].astype(BF16)


def _mem_kv(mem2d, p):
    rows = mem2d.shape[0]
    width = CROSS_HEADS * CROSS_HEAD_DIM
    tm = min(512, rows)
    return pl.pallas_call(
        _mem_kv_kernel,
        grid=(rows // tm,),
        in_specs=[pl.BlockSpec((tm, D_MODEL), lambda i: (i, 0)), _const_spec(p["mem_gain"].shape),
                  _const_spec(p["w_mem_kv"].shape), _const_spec(p["ck_gain"].shape)],
        out_specs=[pl.BlockSpec((tm, width), lambda i: (i, 0))] * 2,
        out_shape=[jax.ShapeDtypeStruct((rows, width), BF16)] * 2,
        compiler_params=pltpu.CompilerParams(dimension_semantics=("parallel",),
                                             vmem_limit_bytes=VMEM_LIMIT),
        name="mem_kv",
    )(mem2d, p["mem_gain"], p["w_mem_kv"], p["ck_gain"])


FF_CHUNK = 1024


def _post_kernel(x_ref, od_ref, om_ref, xq_ref, ck_ref, cv_ref, g_ref,
                 wdo_ref, wmo_ref, wco_ref, wout_ref, gmlp_ref, w1_ref, w2_ref, o_ref):
    heads = []
    for hd in range(CROSS_HEADS):
        sl = slice(hd * LANES, (hd + 1) * LANES)
        s = _dot_nt(xq_ref[:, sl], ck_ref[:, sl])
        p = jnp.exp2(s - jnp.max(s, axis=-1, keepdims=True))
        l = jnp.sum(p, axis=-1, keepdims=True)
        heads.append((_dot(p.astype(BF16), cv_ref[:, sl]) / l).astype(BF16))
    oc = jnp.concatenate(heads, axis=1)

    merged = (g_ref[:, 0:D_MODEL].astype(F32) * _dot(od_ref[...], wdo_ref[...])
              + g_ref[:, D_MODEL:2 * D_MODEL].astype(F32) * _dot(om_ref[...], wmo_ref[...])
              + g_ref[:, 2 * D_MODEL:].astype(F32) * _dot(oc, wco_ref[...]))
    x = x_ref[...] + _dot(merged.astype(BF16), wout_ref[...])

    h = (x * _inv_rms(x, D_MODEL) * gmlp_ref[...]).astype(BF16)
    y = x
    for c in range(D_FF // FF_CHUNK):
        sl = slice(c * FF_CHUNK, (c + 1) * FF_CHUNK)
        u = jnp.maximum(_dot(h, w1_ref[:, sl]), 0.0)
        y = y + _dot((u * u).astype(BF16), w2_ref[sl, :])
    o_ref[...] = y


def _post(x, od, om, xq, ck, cv, gates, p, seq):
    t = x.shape[0]
    tm = min(TM_OUT, seq)
    tiles_per_batch = seq // tm
    width = CROSS_HEADS * CROSS_HEAD_DIM
    row = lambda w: pl.BlockSpec((tm, w), lambda i: (i, 0))
    mem_blk = pl.BlockSpec((MEM_LEN, width), lambda i: (i // tiles_per_batch, 0))
    consts = [p["w_diff_o"], p["w_mla_o"], p["w_cross_o"], p["w_out"], p["g_mlp"], p["w_mlp1"], p["w_mlp2"]]
    return pl.pallas_call(
        _post_kernel,
        grid=(t // tm,),
        in_specs=[row(D_MODEL), row(od.shape[1]), row(om.shape[1]), row(width), mem_blk, mem_blk,
                  row(gates.shape[1])] + [_const_spec(c.shape) for c in consts],
        out_specs=row(D_MODEL),
        out_shape=jax.ShapeDtypeStruct((t, D_MODEL), F32),
        compiler_params=pltpu.CompilerParams(dimension_semantics=("parallel",),
                                             vmem_limit_bytes=VMEM_LIMIT),
        name="post",
    )(x, od, om, xq, ck, cv, gates, *consts)


def _pack_params(w_in, b_gate, g_mix, diff_q_gain, diff_k_gain, diff_lambda, diff_out_gain, w_diff_o,
                 mla_qa_gain, w_mla_qb, mla_kva_gain, w_mla_kvb, mla_q_gain, mla_k_gain, w_mla_o,
                 mem_gain, w_mem_kv, cross_q_gain, cross_k_gain, w_cross_o, w_out, g_mlp, w_mlp1, w_mlp2):
    nl = w_in.shape[0]
    d = D_MODEL
    half = DIFF_HEAD_DIM // 2
    o_dq, o_dk, o_dv = 0, 1024, 2048
    o_cq = 3072
    o_ckv = o_cq + MLA_Q_LORA
    o_kr = o_ckv + MLA_KV_LORA
    o_xq = o_kr + MLA_ROPE_DIM

    def diff_cols(w):
        r = w.shape[1]
        return w.reshape(nl, r, DIFF_HEADS, 2, 2, half // 2, 2).transpose(0, 1, 2, 4, 3, 6, 5).reshape(
            nl, r, DIFF_HEADS * LANES)

    def diff_gain(g, scale):
        g = g.reshape(nl, 2, 1, half // 2, 2).transpose(0, 1, 2, 4, 3)
        g = jnp.broadcast_to(g, (nl, 2, 2, 2, half // 2)).reshape(nl, 1, LANES)
        return (g * scale).astype(F32)

    rh = MLA_ROPE_DIM // 2

    def mla_head_layout(nope, rope):
        pad = jnp.zeros(nope.shape[:-1] + (LANES - MLA_QK_DIM,), nope.dtype)
        return jnp.concatenate([rope[..., :rh], nope[..., :HALF_LANES - rh], rope[..., rh:],
                                nope[..., HALF_LANES - rh:], pad], axis=-1)

    kr_w = w_in[:, :, o_kr:o_xq]
    zpad = jnp.zeros((nl, d, HALF_LANES - rh), F32)
    kr_cols = jnp.concatenate([kr_w[..., :rh], zpad, kr_w[..., rh:], zpad], axis=-1)
    w_in_p = jnp.concatenate([diff_cols(w_in[:, :, o_dq:o_dk]).astype(BF16),
                              diff_cols(w_in[:, :, o_dk:o_dv]).astype(BF16),
                              w_in[:, :, o_dv:o_kr].astype(BF16), kr_cols.astype(BF16),
                              w_in[:, :, o_xq:].astype(BF16)], axis=-1)

    wqb = w_mla_qb.reshape(nl, MLA_Q_LORA, MLA_HEADS, MLA_QK_DIM)
    w_qb_p = mla_head_layout(wqb[..., :MLA_NOPE_DIM], wqb[..., MLA_NOPE_DIM:]).reshape(
        nl, MLA_Q_LORA, MLA_HEADS * LANES).astype(BF16)
    wkvb = w_mla_kvb.reshape(nl, MLA_KV_LORA, MLA_HEADS, MLA_NOPE_DIM + MLA_V_DIM)
    k_nope = wkvb[..., :MLA_NOPE_DIM]
    w_kvk_p = mla_head_layout(k_nope, jnp.zeros(k_nope.shape[:-1] + (MLA_ROPE_DIM,), F32)).reshape(
        nl, MLA_KV_LORA, MLA_HEADS * LANES).astype(BF16)
    w_kvv_p = wkvb[..., MLA_NOPE_DIM:].reshape(nl, MLA_KV_LORA, MLA_HEADS * MLA_V_DIM).astype(BF16)

    def mla_gain(g, scale):
        return (mla_head_layout(g[:, :MLA_NOPE_DIM], g[:, MLA_NOPE_DIM:]) * scale)[:, None, :].astype(F32)

    lam_init = jnp.asarray([0.8 - 0.6 * math.exp(-0.3 * l) for l in range(nl)], F32)
    lam_tab = jnp.zeros((nl, SUBLANES, LANES), F32)
    lam_tab = lam_tab.at[:, 0:4, 0:DIFF_HEAD_DIM].set(diff_lambda.astype(F32))
    lam_tab = lam_tab.at[:, 4, :].set(lam_init[:, None])
    out_gain_rows = jnp.broadcast_to(diff_out_gain[:, :, None], (nl, 2 * DIFF_HEAD_DIM, LANES)).astype(F32)

    row = lambda g: g[:, None, :]
    return dict(
        g_mix=row(g_mix), w_in=w_in_p, b_gate=row(b_gate),
        dq_gain=diff_gain(diff_q_gain, DIFF_HEAD_DIM ** -0.5 * LOG2E), dk_gain=diff_gain(diff_k_gain, 1.0),
        qa_gain=row(mla_qa_gain), w_qb=w_qb_p, kva_gain=row(mla_kva_gain),
        w_kvk=w_kvk_p, w_kvv=w_kvv_p,
        mq_gain=mla_gain(mla_q_gain, MLA_QK_DIM ** -0.5 * LOG2E), mk_gain=mla_gain(mla_k_gain, 1.0),
        xq_gain=row(cross_q_gain * (CROSS_HEAD_DIM ** -0.5 * LOG2E)),
        lam_tab=lam_tab, out_gain_rows=out_gain_rows,
        w_diff_o=w_diff_o.astype(BF16), w_mla_o=w_mla_o.astype(BF16),
        mem_gain=row(mem_gain), w_mem_kv=w_mem_kv.astype(BF16), ck_gain=row(cross_k_gain),
        w_cross_o=w_cross_o.astype(BF16), w_out=w_out.astype(BF16),
        g_mlp=row(g_mlp), w_mlp1=w_mlp1.astype(BF16), w_mlp2=w_mlp2.astype(BF16),
    )


def kernel(x, mem, positions, g_mix, w_in, b_gate, diff_q_gain, diff_k_gain, diff_lambda, diff_out_gain,
           w_diff_o, mla_qa_gain, w_mla_qb, mla_kva_gain, w_mla_kvb, mla_q_gain, mla_k_gain, w_mla_o,
           mem_gain, w_mem_kv, cross_q_gain, cross_k_gain, w_cross_o, w_out, g_mlp, w_mlp1, w_mlp2):
    batch, seq, d = x.shape
    tabs = _rope_tables(positions)
    packed = _pack_params(w_in, b_gate, g_mix, diff_q_gain, diff_k_gain, diff_lambda, diff_out_gain,
                          w_diff_o, mla_qa_gain, w_mla_qb, mla_kva_gain, w_mla_kvb, mla_q_gain, mla_k_gain,
                          w_mla_o, mem_gain, w_mem_kv, cross_q_gain, cross_k_gain, w_cross_o, w_out,
                          g_mlp, w_mlp1, w_mlp2)
    xt = x.reshape(batch * seq, d)
    mem2d = mem.reshape(batch * mem.shape[1], d)
    for l in range(w_in.shape[0]):
        p = {name: a[l] for name, a in packed.items()}
        dq, dk, dv, mq, mk, mv, xq, gates = _inproj(xt, tabs, p)
        od = _diff_attn(p["lam_tab"], p["out_gain_rows"], dq, dk, dv, batch, seq)
        om = _mla_attn(mq, mk, mv, batch, seq)
        ck, cv = _mem_kv(mem2d, p)
        xt = _post(xt, od, om, xq, ck, cv, gates, p, seq)
    return xt.reshape(batch, seq, d)
```

```python
import math

import jax
import jax.numpy as jnp
from jax import lax
from jax.experimental import pallas as pl
from jax.experimental.pallas import tpu as pltpu

D_MODEL = 1024
DEPTH = 4
MEM_LEN = 256
ROPE_THETA = 10000.0
EPS = 1e-6

DIFF_HEADS = 8
DIFF_HEAD_DIM = 64
MLA_HEADS = 8
MLA_Q_LORA = 384
MLA_KV_LORA = 256
MLA_NOPE_DIM = 64
MLA_ROPE_DIM = 32
MLA_V_DIM = 64
MLA_QK_DIM = MLA_NOPE_DIM + MLA_ROPE_DIM
CROSS_HEADS = 4
CROSS_HEAD_DIM = 128
D_FF = 4 * D_MODEL

LANES = 128
SUBLANES = 8
HALF_LANES = LANES // 2
LOG2E = math.log2(math.e)

C_DQ = 0
C_DK = C_DQ + DIFF_HEADS * LANES
C_DV = C_DK + DIFF_HEADS * LANES
C_CQ = C_DV + DIFF_HEADS * LANES
C_CKV = C_CQ + MLA_Q_LORA
C_KR = C_CKV + MLA_KV_LORA
C_XQ = C_KR + LANES
C_GATE = C_XQ + CROSS_HEADS * CROSS_HEAD_DIM
IN_WIDTH_PACKED = C_GATE + 3 * D_MODEL

TM_IN = 512
TM_OUT = 512
TQ = 256
ATTN_GROUPS = 2
PV_KEYS = 256
PV_COLS = 256
VMEM_LIMIT = 56 * 1024 * 1024
ATTN_FLAGS = None

BF16 = jnp.bfloat16
F32 = jnp.float32


def _const_spec(shape):
    nd = len(shape)
    return pl.BlockSpec(shape, lambda *_: (0,) * nd, pipeline_mode=pl.Buffered(1))


def _dot(a, b):
    return jnp.dot(a, b, preferred_element_type=F32)


def _dot_nt(a, b):
    return lax.dot_general(a, b, (((1,), (1,)), ((), ())), preferred_element_type=F32)


def _inv_rms(y, n):
    return lax.rsqrt(jnp.sum(y * y, axis=-1, keepdims=True) * (1.0 / n) + EPS)


def _rope64(y, cos, sin):
    return y * cos + pltpu.roll(y, HALF_LANES, axis=1) * sin


ROPE_FREQS = DIFF_HEAD_DIM // 2
ROPE_GROUPS = LANES // ROPE_FREQS


def _rope_table_kernel(pos_ref, invf_ref, cos_ref, sin_ref):
    ang = pos_ref[...].astype(F32) * invf_ref[...]
    cos, sin = jnp.cos(ang), jnp.sin(ang)
    lane = lax.broadcasted_iota(jnp.int32, ang.shape, 1)
    sign = jnp.where(lane < HALF_LANES, -1.0, 1.0)

    def spread(x, k):
        if k:
            x = pltpu.roll(x, LANES - ROPE_FREQS * k, axis=1)
        x = jnp.where(lane < ROPE_FREQS, x, pltpu.roll(x, ROPE_FREQS, axis=1))
        return jnp.where(lane < HALF_LANES, x, pltpu.roll(x, HALF_LANES, axis=1))

    for k in range(ROPE_GROUPS):
        cos_ref[k] = spread(cos, k)
        sin_ref[k] = spread(sin, k) * sign


def _rope_tables(positions):
    t = positions.size
    rows = t // ROPE_GROUPS
    inv_freq = ROPE_THETA ** (-jnp.arange(0, DIFF_HEAD_DIM, 2, dtype=F32) / DIFF_HEAD_DIM)
    inv_freq = jnp.concatenate([inv_freq[0::2], inv_freq[1::2]])
    pos_rep = jnp.repeat(positions.reshape(ROPE_GROUPS, rows).T, ROPE_FREQS, axis=1)
    invf = jnp.tile(inv_freq, ROPE_GROUPS)[None, :]
    tr = min(1024, rows)
    out_blk = pl.BlockSpec((ROPE_GROUPS, tr, LANES), lambda i: (0, i, 0))
    cos, sin = pl.pallas_call(
        _rope_table_kernel,
        grid=(rows // tr,),
        in_specs=[pl.BlockSpec((tr, LANES), lambda i: (i, 0)),
                  pl.BlockSpec((1, LANES), lambda i: (0, 0))],
        out_specs=[out_blk] * 2,
        out_shape=[jax.ShapeDtypeStruct((ROPE_GROUPS, rows, LANES), F32)] * 2,
        name="rope_tables",
    )(pos_rep, invf)
    return cos.reshape(t, LANES), sin.reshape(t, LANES)


def _inproj_kernel(x_ref, cosd_ref, sind_ref,
                   gmix_ref, win_ref, bgate_ref, dqg_ref, dkg_ref,
                   qag_ref, wqb_ref, kvag_ref, wkvk_ref, wkvv_ref, mqg_ref, mkg_ref, xqg_ref,
                   dq_ref, dk_ref, dv_ref, mq_ref, mk_ref, mv_ref, xq_ref, gate_ref):
    x = x_ref[...]
    h = (x * _inv_rms(x, D_MODEL) * gmix_ref[...]).astype(BF16)
    tm = x.shape[0]
    rs = slice(None)

    cosd, sind = cosd_ref[...], sind_ref[...]
    lane = lax.broadcasted_iota(jnp.int32, (tm, LANES), 1)
    is_map0 = (lane & 32) == 0
    is_mla_rope = (lane & (HALF_LANES - MLA_ROPE_DIM // 2)) == 0
    cosm = jnp.where(is_mla_rope, cosd, 1.0)
    sinm = jnp.where(is_mla_rope, sind, 0.0)

    def diff_qk(col0, gain_ref, out_ref):
        y = _dot(h, win_ref[:, col0:col0 + DIFF_HEADS * LANES])
        gain = gain_ref[...]
        for hd in range(DIFF_HEADS):
            yh = y[:, hd * LANES:(hd + 1) * LANES]
            sq = yh * yh
            s_all = jnp.sum(sq, axis=-1, keepdims=True)
            s0 = jnp.sum(jnp.where(is_map0, sq, 0.0), axis=-1, keepdims=True)
            r0 = lax.rsqrt(s0 * (1.0 / DIFF_HEAD_DIM) + EPS)
            r1 = lax.rsqrt((s_all - s0) * (1.0 / DIFF_HEAD_DIM) + EPS)
            yn = yh * jnp.where(is_map0, r0, r1) * gain
            out_ref[rs, hd * LANES:(hd + 1) * LANES] = _rope64(yn, cosd, sind).astype(BF16)

    def diff_v():
        dv_ref[rs, :] = _dot(h, win_ref[:, C_DV:C_DV + DIFF_HEADS * LANES]).astype(BF16)

    def gate(br):
        sl = slice(br * D_MODEL, (br + 1) * D_MODEL)
        z = _dot(h, win_ref[:, C_GATE + br * D_MODEL:C_GATE + (br + 1) * D_MODEL]) + bgate_ref[:, sl]
        gate_ref[rs, sl] = (1.0 / (1.0 + jnp.exp(-z))).astype(BF16)

    diff_qk(C_DQ, dqg_ref, dq_ref)
    diff_qk(C_DK, dkg_ref, dk_ref)
    diff_v()

    lat = _dot(h, win_ref[:, C_CQ:C_XQ])
    cq = lat[:, 0:MLA_Q_LORA]
    ckv = lat[:, MLA_Q_LORA:MLA_Q_LORA + MLA_KV_LORA]
    kr = lat[:, MLA_Q_LORA + MLA_KV_LORA:]
    cqn = (cq * _inv_rms(cq, MLA_Q_LORA) * qag_ref[...]).astype(BF16)
    ckvn = (ckv * _inv_rms(ckv, MLA_KV_LORA) * kvag_ref[...]).astype(BF16)
    kr = _rope64(kr, cosm, sinm)

    q = _dot(cqn, wqb_ref[...])
    kn = _dot(ckvn, wkvk_ref[...])
    mv_ref[rs, :] = _dot(ckvn, wkvv_ref[...]).astype(BF16)
    mqg, mkg = mqg_ref[...], mkg_ref[...]
    for hd in range(MLA_HEADS):
        sl = slice(hd * LANES, (hd + 1) * LANES)
        qh = _rope64(q[:, sl], cosm, sinm)
        mq_ref[rs, sl] = (qh * _inv_rms(qh, MLA_QK_DIM) * mqg).astype(BF16)
        kh = kn[:, sl] + kr
        mk_ref[rs, sl] = (kh * _inv_rms(kh, MLA_QK_DIM) * mkg).astype(BF16)

    xq = _dot(h, win_ref[:, C_XQ:C_GATE])
    xqg = xqg_ref[...]
    for hd in range(CROSS_HEADS):
        sl = slice(hd * LANES, (hd + 1) * LANES)
        yh = xq[:, sl]
        xq_ref[rs, sl] = (yh * _inv_rms(yh, CROSS_HEAD_DIM) * xqg).astype(BF16)
    for br in range(3):
        gate(br)


def _inproj(x, tabs, p):
    t = x.shape[0]
    tm = min(TM_IN, t)
    row = lambda w: pl.BlockSpec((tm, w), lambda i: (i, 0))
    consts = [p["g_mix"], p["w_in"], p["b_gate"], p["dq_gain"], p["dk_gain"],
              p["qa_gain"], p["w_qb"], p["kva_gain"], p["w_kvk"], p["w_kvv"],
              p["mq_gain"], p["mk_gain"], p["xq_gain"]]
    out_widths = [DIFF_HEADS * LANES] * 3 + [MLA_HEADS * LANES] * 2 + [MLA_HEADS * MLA_V_DIM,
                                                                      CROSS_HEADS * CROSS_HEAD_DIM, 3 * D_MODEL]
    return pl.pallas_call(
        _inproj_kernel,
        grid=(t // tm,),
        in_specs=[row(D_MODEL)] + [row(LANES)] * 2 + [_const_spec(c.shape) for c in consts],
        out_specs=[row(w) for w in out_widths],
        out_shape=[jax.ShapeDtypeStruct((t, w), BF16) for w in out_widths],
        compiler_params=pltpu.CompilerParams(dimension_semantics=("parallel",),
                                             vmem_limit_bytes=VMEM_LIMIT),
        name="inproj",
    )(x, *tabs, *consts)


def _causal_bias_t(tq):
    r = lax.broadcasted_iota(jnp.int32, (tq, tq), 0)
    c = lax.broadcasted_iota(jnp.int32, (tq, tq), 1)
    return jnp.where(r <= c, 0.0, -jnp.inf).astype(F32)


def _fold_rows(a, op):
    n, w = a.shape
    return op(op(a.reshape(n // SUBLANES, SUBLANES, w), axis=0), axis=0, keepdims=True)


def _causal_attention(jobs, tq, k_ref, s_ref):
    bias_t = _causal_bias_t(tq)
    n_jobs = len(jobs)

    def scores(j):
        i, make_kq, _, _ = jobs[j]
        lo, hi = i * tq, (i + 1) * tq
        s_buf = s_ref.at[j % 2]
        maxima = []
        for k_sl, q_rows, s_sl in make_kq():
            reps = q_rows.shape[0] // tq
            bias = bias_t if reps == 1 else jnp.concatenate([bias_t] * reps, axis=1)
            s = _dot_nt(k_ref[lo:hi, k_sl], q_rows) + bias
            s_buf[lo:hi, s_sl] = s
            m = _fold_rows(s, jnp.max)
            if lo > 0:
                s = _dot_nt(k_ref[0:lo, k_sl], q_rows)
                s_buf[0:lo, s_sl] = s
                m = jnp.maximum(m, _fold_rows(s, jnp.max))
            maxima.append(m)
        return maxima[0] if len(maxima) == 1 else jnp.concatenate(maxima, axis=1)

    def softmax_pv(j, m):
        i, _, vt_ref, finish = jobs[j]
        s_buf = s_ref.at[j % 2]
        n_col = s_buf.shape[1] // PV_COLS
        acc = [None] * n_col
        for r in range(0, (i + 1) * tq, PV_KEYS):
            for n in range(n_col):
                cols = slice(n * PV_COLS, (n + 1) * PV_COLS)
                p = jnp.exp2(s_buf[r:r + PV_KEYS, cols] - m[:, cols]).astype(BF16)
                part = _dot(vt_ref[:, r:r + PV_KEYS], p)
                acc[n] = part if acc[n] is None else acc[n] + part
        acc = jnp.concatenate(acc, axis=1)
        finish(acc[0:LANES], acc[LANES:LANES + 1])

    m_next = scores(0)
    for j in range(n_jobs):
        m_cur = m_next
        if j + 1 < n_jobs:
            m_next = scores(j + 1)
        softmax_pv(j, m_cur)


def _job_order(job, n_groups, n_tiles):
    return [job(g, i) for i in range(n_tiles) for g in range(n_groups)]


VT_ROWS = LANES + 16


def _store_vt(v_ref, vt_ref):
    for g in range(vt_ref.shape[0]):
        vt_ref[g, 0:LANES, :] = v_ref[:, g * LANES:(g + 1) * LANES].astype(F32).T.astype(BF16)
        vt_ref[g, LANES:VT_ROWS, :] = jnp.ones((VT_ROWS - LANES, vt_ref.shape[2]), BF16)


def _diff_attn_kernel(lam_ref, og_ref, q_ref, k_ref, v_ref, o_ref, vt_ref, s_ref):
    s_len = q_ref.shape[0]
    tq = TQ
    lane = lax.broadcasted_iota(jnp.int32, (tq, LANES), 1)
    is_map0 = (lane & 32) == 0
    _store_vt(v_ref, vt_ref)

    lv = lam_ref[...]
    lam = (jnp.exp(jnp.sum(lv[0:1] * lv[1:2], axis=-1, keepdims=True))
           - jnp.exp(jnp.sum(lv[2:3] * lv[3:4], axis=-1, keepdims=True))
           + lv[4:5, 0:1])
    og = og_ref[...] * (1.0 - lv[4:5, 0:1])
    og = jnp.concatenate([og] * (tq // LANES), axis=1)

    def job(g, i):
        cols = slice(g * LANES, (g + 1) * LANES)
        rows = slice(i * tq, (i + 1) * tq)

        def make_kq():
            q = q_ref[rows, cols]
            zero = jnp.zeros_like(q)
            qq = jnp.concatenate([jnp.where(is_map0, q, zero), jnp.where(is_map0, zero, q)], axis=0)
            return [(cols, qq, slice(None))]

        def finish(acc, l):
            o = acc / l
            o = o[:, 0:tq] - lam * o[:, tq:2 * tq]
            r = lax.rsqrt(jnp.sum(o * o, axis=0, keepdims=True) * (1.0 / (2 * DIFF_HEAD_DIM)) + EPS)
            o_ref[rows, cols] = (o * r * og).T.astype(BF16)

        return i, make_kq, vt_ref.at[g], finish

    jobs = _job_order(job, vt_ref.shape[0], s_len // tq)
    _causal_attention(jobs, tq, k_ref, s_ref)


def _attn_scratch(seq):
    return [pltpu.VMEM((ATTN_GROUPS, VT_ROWS, seq), BF16),
            pltpu.VMEM((2, seq, 2 * TQ), F32)]


def _diff_attn(lam_tab, out_gain_rows, dq, dk, dv, batch, seq):
    blk = pl.BlockSpec((None, seq, ATTN_GROUPS * LANES), lambda b, h: (b, 0, h))
    shp = (batch, seq, DIFF_HEADS * LANES)
    out = pl.pallas_call(
        _diff_attn_kernel,
        grid=(batch, DIFF_HEADS // ATTN_GROUPS),
        in_specs=[_const_spec(lam_tab.shape), _const_spec(out_gain_rows.shape), blk, blk, blk],
        out_specs=blk,
        out_shape=jax.ShapeDtypeStruct(shp, BF16),
        scratch_shapes=_attn_scratch(seq),
        compiler_params=pltpu.CompilerParams(dimension_semantics=("parallel", "parallel"),
                                             vmem_limit_bytes=VMEM_LIMIT, flags=ATTN_FLAGS),
        name="diff_attn",
    )(lam_tab, out_gain_rows, dq.reshape(shp), dk.reshape(shp), dv.reshape(shp))
    return out.reshape(batch * seq, DIFF_HEADS * LANES)


def _mla_attn_kernel(q_ref, k_ref, v_ref, o_ref, vt_ref, s_ref):
    s_len = q_ref.shape[0]
    tq = TQ
    _store_vt(v_ref, vt_ref)
    row = lax.broadcasted_iota(jnp.int32, (LANES, tq), 0)
    is_head0 = row < MLA_V_DIM

    def job(g, i):
        rows = slice(i * tq, (i + 1) * tq)

        def make_kq():
            heads = [slice((2 * g + hd) * LANES, (2 * g + hd + 1) * LANES) for hd in range(2)]
            return [(heads[hd], q_ref[rows, heads[hd]], slice(hd * tq, (hd + 1) * tq)) for hd in range(2)]

        def finish(acc, l):
            o = acc / l
            o = jnp.where(is_head0, o[:, 0:tq], o[:, tq:2 * tq])
            o_ref[rows, g * LANES:(g + 1) * LANES] = o.T.astype(BF16)

        return i, make_kq, vt_ref.at[g], finish

    jobs = _job_order(job, vt_ref.shape[0], s_len // tq)
    _causal_attention(jobs, tq, k_ref, s_ref)


def _mla_attn(mq, mk, mv, batch, seq):
    pairs = ATTN_GROUPS
    qk_blk = pl.BlockSpec((None, seq, pairs * 2 * LANES), lambda b, h: (b, 0, h))
    v_blk = pl.BlockSpec((None, seq, pairs * 2 * MLA_V_DIM), lambda b, h: (b, 0, h))
    qk_shp = (batch, seq, MLA_HEADS * LANES)
    v_shp = (batch, seq, MLA_HEADS * MLA_V_DIM)
    out = pl.pallas_call(
        _mla_attn_kernel,
        grid=(batch, MLA_HEADS // (2 * pairs)),
        in_specs=[qk_blk, qk_blk, v_blk],
        out_specs=v_blk,
        out_shape=jax.ShapeDtypeStruct(v_shp, BF16),
        scratch_shapes=_attn_scratch(seq),
        compiler_params=pltpu.CompilerParams(dimension_semantics=("parallel", "parallel"),
                                             vmem_limit_bytes=VMEM_LIMIT, flags=ATTN_FLAGS),
        name="mla_attn",
    )(mq.reshape(qk_shp), mk.reshape(qk_shp), mv.reshape(v_shp))
    return out.reshape(batch * seq, MLA_HEADS * MLA_V_DIM)


def _mem_kv_kernel(mem_ref, mg_ref, w_ref, kg_ref, k_ref, v_ref):
    m = mem_ref[...]
    mn = (m * _inv_rms(m, D_MODEL) * mg_ref[...]).astype(BF16)
    kv = _dot(mn, w_ref[...])
    width = CROSS_HEADS * CROSS_HEAD_DIM
    kg = kg_ref[...]
    for hd in range(CROSS_HEADS):
        sl = slice(hd * LANES, (hd + 1) * LANES)
        kh = kv[:, sl]
        k_ref[:, sl] = (kh * _inv_rms(kh, CROSS_HEAD_DIM) * kg).astype(BF16)
    v_ref[...] = kv[:, width:].astype(BF16)


def _mem_kv(mem2d, p):
    rows = mem2d.shape[0]
    width = CROSS_HEADS * CROSS_HEAD_DIM
    tm = min(512, rows)
    return pl.pallas_call(
        _mem_kv_kernel,
        grid=(rows // tm,),
        in_specs=[pl.BlockSpec((tm, D_MODEL), lambda i: (i, 0)), _const_spec(p["mem_gain"].shape),
                  _const_spec(p["w_mem_kv"].shape), _const_spec(p["ck_gain"].shape)],
        out_specs=[pl.BlockSpec((tm, width), lambda i: (i, 0))] * 2,
        out_shape=[jax.ShapeDtypeStruct((rows, width), BF16)] * 2,
        compiler_params=pltpu.CompilerParams(dimension_semantics=("parallel",),
                                             vmem_limit_bytes=VMEM_LIMIT),
        name="mem_kv",
    )(mem2d, p["mem_gain"], p["w_mem_kv"], p["ck_gain"])


FF_CHUNK = 1024


def _post_kernel(x_ref, od_ref, om_ref, xq_ref, ck_ref, cv_ref, g_ref,
                 wdo_ref, wmo_ref, wco_ref, wout_ref, gmlp_ref, w1_ref, w2_ref, o_ref):
    heads = []
    for hd in range(CROSS_HEADS):
        sl = slice(hd * LANES, (hd + 1) * LANES)
        s = _dot_nt(xq_ref[:, sl], ck_ref[:, sl])
        p = jnp.exp2(s - jnp.max(s, axis=-1, keepdims=True))
        l = jnp.sum(p, axis=-1, keepdims=True)
        heads.append((_dot(p.astype(BF16), cv_ref[:, sl]) / l).astype(BF16))
    oc = jnp.concatenate(heads, axis=1)

    merged = (g_ref[:, 0:D_MODEL].astype(F32) * _dot(od_ref[...], wdo_ref[...])
              + g_ref[:, D_MODEL:2 * D_MODEL].astype(F32) * _dot(om_ref[...], wmo_ref[...])
              + g_ref[:, 2 * D_MODEL:].astype(F32) * _dot(oc, wco_ref[...]))
    x = x_ref[...] + _dot(merged.astype(BF16), wout_ref[...])

    h = (x * _inv_rms(x, D_MODEL) * gmlp_ref[...]).astype(BF16)
    y = x
    for c in range(D_FF // FF_CHUNK):
        sl = slice(c * FF_CHUNK, (c + 1) * FF_CHUNK)
        u = jnp.maximum(_dot(h, w1_ref[:, sl]), 0.0)
        y = y + _dot((u * u).astype(BF16), w2_ref[sl, :])
    o_ref[...] = y


def _post(x, od, om, xq, ck, cv, gates, p, seq):
    t = x.shape[0]
    tm = min(TM_OUT, seq)
    tiles_per_batch = seq // tm
    width = CROSS_HEADS * CROSS_HEAD_DIM
    row = lambda w: pl.BlockSpec((tm, w), lambda i: (i, 0))
    mem_blk = pl.BlockSpec((MEM_LEN, width), lambda i: (i // tiles_per_batch, 0))
    consts = [p["w_diff_o"], p["w_mla_o"], p["w_cross_o"], p["w_out"], p["g_mlp"], p["w_mlp1"], p["w_mlp2"]]
    return pl.pallas_call(
        _post_kernel,
        grid=(t // tm,),
        in_specs=[row(D_MODEL), row(od.shape[1]), row(om.shape[1]), row(width), mem_blk, mem_blk,
                  row(gates.shape[1])] + [_const_spec(c.shape) for c in consts],
        out_specs=row(D_MODEL),
        out_shape=jax.ShapeDtypeStruct((t, D_MODEL), F32),
        compiler_params=pltpu.CompilerParams(dimension_semantics=("parallel",),
                                             vmem_limit_bytes=VMEM_LIMIT),
        name="post",
    )(x, od, om, xq, ck, cv, gates, *consts)


def _pack_params(w_in, b_gate, g_mix, diff_q_gain, diff_k_gain, diff_lambda, diff_out_gain, w_diff_o,
                 mla_qa_gain, w_mla_qb, mla_kva_gain, w_mla_kvb, mla_q_gain, mla_k_gain, w_mla_o,
                 mem_gain, w_mem_kv, cross_q_gain, cross_k_gain, w_cross_o, w_out, g_mlp, w_mlp1, w_mlp2):
    nl = w_in.shape[0]
    d = D_MODEL
    half = DIFF_HEAD_DIM // 2
    o_dq, o_dk, o_dv = 0, 1024, 2048
    o_cq = 3072
    o_ckv = o_cq + MLA_Q_LORA
    o_kr = o_ckv + MLA_KV_LORA
    o_xq = o_kr + MLA_ROPE_DIM

    def diff_cols(w):
        r = w.shape[1]
        return w.reshape(nl, r, DIFF_HEADS, 2, 2, half // 2, 2).transpose(0, 1, 2, 4, 3, 6, 5).reshape(
            nl, r, DIFF_HEADS * LANES)

    def diff_gain(g, scale):
        g = g.reshape(nl, 2, 1, half // 2, 2).transpose(0, 1, 2, 4, 3)
        g = jnp.broadcast_to(g, (nl, 2, 2, 2, half // 2)).reshape(nl, 1, LANES)
        return (g * scale).astype(F32)

    rh = MLA_ROPE_DIM // 2

    def mla_head_layout(nope, rope):
        pad = jnp.zeros(nope.shape[:-1] + (LANES - MLA_QK_DIM,), nope.dtype)
        return jnp.concatenate([rope[..., :rh], nope[..., :HALF_LANES - rh], rope[..., rh:],
                                nope[..., HALF_LANES - rh:], pad], axis=-1)

    kr_w = w_in[:, :, o_kr:o_xq]
    zpad = jnp.zeros((nl, d, HALF_LANES - rh), F32)
    kr_cols = jnp.concatenate([kr_w[..., :rh], zpad, kr_w[..., rh:], zpad], axis=-1)
    w_in_p = jnp.concatenate([diff_cols(w_in[:, :, o_dq:o_dk]).astype(BF16),
                              diff_cols(w_in[:, :, o_dk:o_dv]).astype(BF16),
                              w_in[:, :, o_dv:o_kr].astype(BF16), kr_cols.astype(BF16),
                              w_in[:, :, o_xq:].astype(BF16)], axis=-1)

    wqb = w_mla_qb.reshape(nl, MLA_Q_LORA, MLA_HEADS, MLA_QK_DIM)
    w_qb_p = mla_head_layout(wqb[..., :MLA_NOPE_DIM], wqb[..., MLA_NOPE_DIM:]).reshape(
        nl, MLA_Q_LORA, MLA_HEADS * LANES).astype(BF16)
    wkvb = w_mla_kvb.reshape(nl, MLA_KV_LORA, MLA_HEADS, MLA_NOPE_DIM + MLA_V_DIM)
    k_nope = wkvb[..., :MLA_NOPE_DIM]
    w_kvk_p = mla_head_layout(k_nope, jnp.zeros(k_nope.shape[:-1] + (MLA_ROPE_DIM,), F32)).reshape(
        nl, MLA_KV_LORA, MLA_HEADS * LANES).astype(BF16)
    w_kvv_p = wkvb[..., MLA_NOPE_DIM:].reshape(nl, MLA_KV_LORA, MLA_HEADS * MLA_V_DIM).astype(BF16)

    def mla_gain(g, scale):
        return (mla_head_layout(g[:, :MLA_NOPE_DIM], g[:, MLA_NOPE_DIM:]) * scale)[:, None, :].astype(F32)

    lam_init = jnp.asarray([0.8 - 0.6 * math.exp(-0.3 * l) for l in range(nl)], F32)
    lam_tab = jnp.zeros((nl, SUBLANES, LANES), F32)
    lam_tab = lam_tab.at[:, 0:4, 0:DIFF_HEAD_DIM].set(diff_lambda.astype(F32))
    lam_tab = lam_tab.at[:, 4, :].set(lam_init[:, None])
    out_gain_rows = jnp.broadcast_to(diff_out_gain[:, :, None], (nl, 2 * DIFF_HEAD_DIM, LANES)).astype(F32)

    row = lambda g: g[:, None, :]
    return dict(
        g_mix=row(g_mix), w_in=w_in_p, b_gate=row(b_gate),
        dq_gain=diff_gain(diff_q_gain, DIFF_HEAD_DIM ** -0.5 * LOG2E), dk_gain=diff_gain(diff_k_gain, 1.0),
        qa_gain=row(mla_qa_gain), w_qb=w_qb_p, kva_gain=row(mla_kva_gain),
        w_kvk=w_kvk_p, w_kvv=w_kvv_p,
        mq_gain=mla_gain(mla_q_gain, MLA_QK_DIM ** -0.5 * LOG2E), mk_gain=mla_gain(mla_k_gain, 1.0),
        xq_gain=row(cross_q_gain * (CROSS_HEAD_DIM ** -0.5 * LOG2E)),
        lam_tab=lam_tab, out_gain_rows=out_gain_rows,
        w_diff_o=w_diff_o.astype(BF16), w_mla_o=w_mla_o.astype(BF16),
        mem_gain=row(mem_gain), w_mem_kv=w_mem_kv.astype(BF16), ck_gain=row(cross_k_gain),
        w_cross_o=w_cross_o.astype(BF16), w_out=w_out.astype(BF16),
        g_mlp=row(g_mlp), w_mlp1=w_mlp1.astype(BF16), w_mlp2=w_mlp2.astype(BF16),
    )


def kernel(x, mem, positions, g_mix, w_in, b_gate, diff_q_gain, diff_k_gain, diff_lambda, diff_out_gain,
           w_diff_o, mla_qa_gain, w_mla_qb, mla_kva_gain, w_mla_kvb, mla_q_gain, mla_k_gain, w_mla_o,
           mem_gain, w_mem_kv, cross_q_gain, cross_k_gain, w_cross_o, w_out, g_mlp, w_mlp1, w_mlp2):
    batch, seq, d = x.shape
    tabs = _rope_tables(positions)
    packed = _pack_params(w_in, b_gate, g_mix, diff_q_gain, diff_k_gain, diff_lambda, diff_out_gain,
                          w_diff_o, mla_qa_gain, w_mla_qb, mla_kva_gain, w_mla_kvb, mla_q_gain, mla_k_gain,
                          w_mla_o, mem_gain, w_mem_kv, cross_q_gain, cross_k_gain, w_cross_o, w_out,
                          g_mlp, w_mlp1, w_mlp2)
    xt = x.reshape(batch * seq, d)
    mem2d = mem.reshape(batch * mem.shape[1], d)
    for l in range(w_in.shape[0]):
        p = {name: a[l] for name, a in packed.items()}
        dq, dk, dv, mq, mk, mv, xq, gates = _inproj(xt, tabs, p)
        od = _diff_attn(p["lam_tab"], p["out_gain_rows"], dq, dk, dv, batch, seq)
        om = _mla_attn(mq, mk, mv, batch, seq)
        ck, cv = _mem_kv(mem2d, p)
        xt = _post(xt, od, om, xq, ck, cv, gates, p, seq)
    return xt.reshape(batch, seq, d)
```

```python
import math

import jax
import jax.numpy as jnp
from jax import lax
from jax.experimental import pallas as pl
from jax.experimental.pallas import tpu as pltpu

D_MODEL = 1024
DEPTH = 4
MEM_LEN = 256
ROPE_THETA = 10000.0
EPS = 1e-6

DIFF_HEADS = 8
DIFF_HEAD_DIM = 64
MLA_HEADS = 8
MLA_Q_LORA = 384
MLA_KV_LORA = 256
MLA_NOPE_DIM = 64
MLA_ROPE_DIM = 32
MLA_V_DIM = 64
MLA_QK_DIM = MLA_NOPE_DIM + MLA_ROPE_DIM
CROSS_HEADS = 4
CROSS_HEAD_DIM = 128
D_FF = 4 * D_MODEL

LANES = 128
SUBLANES = 8
HALF_LANES = LANES // 2
LOG2E = math.log2(math.e)

W_HEADS = DIFF_HEADS * LANES
C_DQ = 0
C_DK = C_DQ + W_HEADS
C_DV = C_DK + W_HEADS
C_CQ = C_DV + W_HEADS
C_CKV = C_CQ + MLA_Q_LORA
C_KR = C_CKV + MLA_KV_LORA
C_XQ = C_KR + LANES
C_GATE = C_XQ + CROSS_HEADS * CROSS_HEAD_DIM
IN_WIDTH_PACKED = C_GATE + 3 * D_MODEL

TM_IN = 512
TM_OUT = 512
TQ = 256
ATTN_GROUPS = 2
SM_ROWS = 32
VMEM_LIMIT = 56 * 1024 * 1024

BF16 = jnp.bfloat16
F32 = jnp.float32


def _layer_spec(shape):
    nd = len(shape) - 1
    return pl.BlockSpec((None,) + tuple(shape[1:]), lambda *args: (args[-1][0],) + (0,) * nd,
                        pipeline_mode=pl.Buffered(1))


def _layer_call(kernel_fn, name, grid, in_specs, out_specs, out_shape, scratch_shapes=()):
    return pl.pallas_call(
        kernel_fn,
        grid_spec=pltpu.PrefetchScalarGridSpec(num_scalar_prefetch=1, grid=grid, in_specs=in_specs,
                                               out_specs=out_specs, scratch_shapes=scratch_shapes),
        out_shape=out_shape,
        compiler_params=pltpu.CompilerParams(dimension_semantics=("parallel",) * len(grid),
                                             vmem_limit_bytes=VMEM_LIMIT),
        name=name)


def _dot(a, b):
    return jnp.dot(a, b, preferred_element_type=F32)


def _dot_nt(a, b):
    return lax.dot_general(a, b, (((1,), (1,)), ((), ())), preferred_element_type=F32)


def _inv_rms(y, n):
    return lax.rsqrt(jnp.sum(y * y, axis=-1, keepdims=True) * (1.0 / n) + EPS)


def _rope64(y, cos, sin):
    return y * cos + pltpu.roll(y, HALF_LANES, axis=1) * sin


ROPE_FREQS = DIFF_HEAD_DIM // 2
ROPE_GROUPS = LANES // ROPE_FREQS


def _rope_table_kernel(pos_ref, invf_ref, cos_ref, sin_ref):
    ang = pos_ref[...].astype(F32) * invf_ref[...]
    cos, sin = jnp.cos(ang), jnp.sin(ang)
    lane = lax.broadcasted_iota(jnp.int32, ang.shape, 1)
    sign = jnp.where(lane < HALF_LANES, -1.0, 1.0)

    def spread(x, k):
        if k:
            x = pltpu.roll(x, LANES - ROPE_FREQS * k, axis=1)
        x = jnp.where(lane < ROPE_FREQS, x, pltpu.roll(x, ROPE_FREQS, axis=1))
        return jnp.where(lane < HALF_LANES, x, pltpu.roll(x, HALF_LANES, axis=1))

    for k in range(ROPE_GROUPS):
        cos_ref[k] = spread(cos, k)
        sin_ref[k] = spread(sin, k) * sign


def _rope_tables(positions):
    t = positions.size
    rows = t // ROPE_GROUPS
    inv_freq = ROPE_THETA ** (-jnp.arange(0, DIFF_HEAD_DIM, 2, dtype=F32) / DIFF_HEAD_DIM)
    inv_freq = jnp.concatenate([inv_freq[0::2], inv_freq[1::2]])
    pos_rep = jnp.repeat(positions.reshape(ROPE_GROUPS, rows).T, ROPE_FREQS, axis=1)
    invf = jnp.tile(inv_freq, ROPE_GROUPS)[None, :]
    tr = min(1024, rows)
    out_blk = pl.BlockSpec((ROPE_GROUPS, tr, LANES), lambda i: (0, i, 0))
    cos, sin = pl.pallas_call(
        _rope_table_kernel,
        grid=(rows // tr,),
        in_specs=[pl.BlockSpec((tr, LANES), lambda i: (i, 0)),
                  pl.BlockSpec((1, LANES), lambda i: (0, 0))],
        out_specs=[out_blk] * 2,
        out_shape=[jax.ShapeDtypeStruct((ROPE_GROUPS, rows, LANES), F32)] * 2,
        name="rope_tables",
    )(pos_rep, invf)
    return cos.reshape(t, LANES), sin.reshape(t, LANES)


def _inproj_kernel(_layer_ref, x_ref, cosd_ref, sind_ref,
                   gmix_ref, win_ref, bgate_ref, dqg_ref, dkg_ref,
                   qag_ref, wqb_ref, kvag_ref, wkvk_ref, wkvv_ref, mqg_ref, mkg_ref, xqg_ref,
                   dq_ref, dk_ref, dv_ref, mq_ref, mk_ref, mv_ref, xq_ref, gate_ref):
    x = x_ref[...]
    h = (x * _inv_rms(x, D_MODEL) * gmix_ref[...]).astype(BF16)
    tm = x.shape[0]
    rs = slice(None)

    cosd, sind = cosd_ref[...], sind_ref[...]
    lane = lax.broadcasted_iota(jnp.int32, (tm, LANES), 1)
    is_map0 = (lane & 32) == 0
    is_mla_rope = (lane & (HALF_LANES - MLA_ROPE_DIM // 2)) == 0
    cosm = jnp.where(is_mla_rope, cosd, 1.0)
    sinm = jnp.where(is_mla_rope, sind, 0.0)

    def diff_qk(col0, gain_ref, out_ref):
        y = _dot(h, win_ref[:, col0:col0 + W_HEADS])
        gain = gain_ref[...]
        for hd in range(DIFF_HEADS):
            yh = y[:, hd * LANES:(hd + 1) * LANES]
            sq = yh * yh
            s_all = jnp.sum(sq, axis=-1, keepdims=True)
            s0 = jnp.sum(jnp.where(is_map0, sq, 0.0), axis=-1, keepdims=True)
            r0 = lax.rsqrt(s0 * (1.0 / DIFF_HEAD_DIM) + EPS)
            r1 = lax.rsqrt((s_all - s0) * (1.0 / DIFF_HEAD_DIM) + EPS)
            yn = yh * jnp.where(is_map0, r0, r1) * gain
            out_ref[rs, hd * LANES:(hd + 1) * LANES] = _rope64(yn, cosd, sind).astype(BF16)

    def diff_v():
        dv_ref[rs, :] = _dot(h, win_ref[:, C_DV:C_DV + W_HEADS]).astype(BF16)

    def gate(br):
        sl = slice(br * D_MODEL, (br + 1) * D_MODEL)
        z = _dot(h, win_ref[:, C_GATE + br * D_MODEL:C_GATE + (br + 1) * D_MODEL]) + bgate_ref[:, sl]
        gate_ref[rs, sl] = (1.0 / (1.0 + jnp.exp(-z))).astype(BF16)

    diff_qk(C_DQ, dqg_ref, dq_ref)
    diff_qk(C_DK, dkg_ref, dk_ref)
    diff_v()

    lat = _dot(h, win_ref[:, C_CQ:C_XQ])
    cq = lat[:, 0:MLA_Q_LORA]
    ckv = lat[:, MLA_Q_LORA:MLA_Q_LORA + MLA_KV_LORA]
    kr = lat[:, MLA_Q_LORA + MLA_KV_LORA:]
    cqn = (cq * _inv_rms(cq, MLA_Q_LORA) * qag_ref[...]).astype(BF16)
    ckvn = (ckv * _inv_rms(ckv, MLA_KV_LORA) * kvag_ref[...]).astype(BF16)
    kr = _rope64(kr, cosm, sinm)

    q = _dot(cqn, wqb_ref[...])
    kn = _dot(ckvn, wkvk_ref[...])
    mv_ref[rs, :] = _dot(ckvn, wkvv_ref[...]).astype(BF16)
    mqg, mkg = mqg_ref[...], mkg_ref[...]
    for hd in range(MLA_HEADS):
        sl = slice(hd * LANES, (hd + 1) * LANES)
        qh = _rope64(q[:, sl], cosm, sinm)
        mq_ref[rs, sl] = (qh * _inv_rms(qh, MLA_QK_DIM) * mqg).astype(BF16)
        kh = kn[:, sl] + kr
        mk_ref[rs, sl] = (kh * _inv_rms(kh, MLA_QK_DIM) * mkg).astype(BF16)

    xq = _dot(h, win_ref[:, C_XQ:C_GATE])
    xqg = xqg_ref[...]
    for hd in range(CROSS_HEADS):
        sl = slice(hd * LANES, (hd + 1) * LANES)
        yh = xq[:, sl]
        xq_ref[rs, sl] = (yh * _inv_rms(yh, CROSS_HEAD_DIM) * xqg).astype(BF16)
    for br in range(3):
        gate(br)


def _inproj(layer, x, tabs, p):
    t = x.shape[0]
    tm = min(TM_IN, t)
    row = lambda w: pl.BlockSpec((tm, w), lambda i, l: (i, 0))
    consts = [p["g_mix"], p["w_in"], p["b_gate"], p["dq_gain"], p["dk_gain"],
              p["qa_gain"], p["w_qb"], p["kva_gain"], p["w_kvk"], p["w_kvv"],
              p["mq_gain"], p["mk_gain"], p["xq_gain"]]
    out_widths = [W_HEADS] * 5 + [MLA_HEADS * MLA_V_DIM, CROSS_HEADS * CROSS_HEAD_DIM, 3 * D_MODEL]
    return _layer_call(
        _inproj_kernel, "inproj", (t // tm,),
        in_specs=[row(D_MODEL)] + [row(LANES)] * 2 + [_layer_spec(c.shape) for c in consts],
        out_specs=[row(w) for w in out_widths],
        out_shape=[jax.ShapeDtypeStruct((t, w), BF16) for w in out_widths],
    )(layer, x, *tabs, *consts)


def _causal_bias_t(tq):
    r = lax.broadcasted_iota(jnp.int32, (tq, tq), 0)
    c = lax.broadcasted_iota(jnp.int32, (tq, tq), 1)
    return jnp.where(r <= c, 0.0, -jnp.inf).astype(F32)


def _fold_rows(a, op):
    n, w = a.shape
    return op(op(a.reshape(n // SUBLANES, SUBLANES, w), axis=0), axis=0, keepdims=True)


def _causal_attention(jobs, tq, k_ref, s_ref, p_ref):
    bias_t = _causal_bias_t(tq)
    n_jobs = len(jobs)

    def scores(j):
        i, make_kq, _, _ = jobs[j]
        lo, hi = i * tq, (i + 1) * tq
        s_buf = s_ref.at[j % 2]
        maxima = []
        for k_sl, q_rows, s_sl in make_kq():
            reps = q_rows.shape[0] // tq
            bias = bias_t if reps == 1 else jnp.concatenate([bias_t] * reps, axis=1)
            s = _dot_nt(k_ref[lo:hi, k_sl], q_rows) + bias
            s_buf[lo:hi, s_sl] = s
            m = _fold_rows(s, jnp.max)
            if lo > 0:
                s = _dot_nt(k_ref[0:lo, k_sl], q_rows)
                s_buf[0:lo, s_sl] = s
                m = jnp.maximum(m, _fold_rows(s, jnp.max))
            maxima.append(m)
        return maxima[0] if len(maxima) == 1 else jnp.concatenate(maxima, axis=1)

    def softmax(j, m):
        hi = (jobs[j][0] + 1) * tq
        s_buf, p_buf = s_ref.at[j % 2], p_ref.at[j % 2]
        for r in range(0, hi, SM_ROWS):
            p_buf[r:r + SM_ROWS, :] = jnp.exp2(s_buf[r:r + SM_ROWS, :] - m).astype(BF16)

    def pv(j):
        i, _, vt_ref, finish = jobs[j]
        hi = (i + 1) * tq
        acc = _dot(vt_ref[:, 0:hi], p_ref[j % 2, 0:hi, :])
        finish(acc[0:LANES], acc[LANES:LANES + 1])

    m_next = scores(0)
    for j in range(n_jobs):
        m_cur = m_next
        if j + 1 < n_jobs:
            m_next = scores(j + 1)
        softmax(j, m_cur)
        if j > 0:
            pv(j - 1)
    pv(n_jobs - 1)


def _job_order(job, n_groups, n_tiles):
    return [job(g, i) for i in range(n_tiles) for g in range(n_groups)]


VT_ROWS = LANES + 16


def _store_vt(v_ref, vt_ref):
    for g in range(vt_ref.shape[0]):
        vt_ref[g, 0:LANES, :] = v_ref[:, g * LANES:(g + 1) * LANES].astype(F32).T.astype(BF16)
        vt_ref[g, LANES:VT_ROWS, :] = jnp.ones((VT_ROWS - LANES, vt_ref.shape[2]), BF16)


def _diff_attn_kernel(_layer_ref, lam_ref, og_ref, q_ref, k_ref, v_ref, o_ref, vt_ref, s_ref, p_ref):
    s_len = q_ref.shape[0]
    tq = TQ
    lane = lax.broadcasted_iota(jnp.int32, (tq, LANES), 1)
    is_map0 = (lane & 32) == 0
    _store_vt(v_ref, vt_ref)

    lv = lam_ref[...]
    lam = (jnp.exp(jnp.sum(lv[0:1] * lv[1:2], axis=-1, keepdims=True))
           - jnp.exp(jnp.sum(lv[2:3] * lv[3:4], axis=-1, keepdims=True))
           + lv[4:5, 0:1])
    og = og_ref[...] * (1.0 - lv[4:5, 0:1])
    og = jnp.concatenate([og] * (tq // LANES), axis=1)

    def job(g, i):
        cols = slice(g * LANES, (g + 1) * LANES)
        rows = slice(i * tq, (i + 1) * tq)

        def make_kq():
            q = q_ref[rows, cols]
            zero = jnp.zeros_like(q)
            qq = jnp.concatenate([jnp.where(is_map0, q, zero), jnp.where(is_map0, zero, q)], axis=0)
            return [(cols, qq, slice(None))]

        def finish(acc, l):
            o = acc / l
            o = o[:, 0:tq] - lam * o[:, tq:2 * tq]
            r = lax.rsqrt(jnp.sum(o * o, axis=0, keepdims=True) * (1.0 / (2 * DIFF_HEAD_DIM)) + EPS)
            o_ref[rows, cols] = (o * r * og).T.astype(BF16)

        return i, make_kq, vt_ref.at[g], finish

    jobs = _job_order(job, vt_ref.shape[0], s_len // tq)
    _causal_attention(jobs, tq, k_ref, s_ref, p_ref)


def _attn_scratch(seq):
    return [pltpu.VMEM((ATTN_GROUPS, VT_ROWS, seq), BF16),
            pltpu.VMEM((2, seq, 2 * TQ), F32),
            pltpu.VMEM((2, seq, 2 * TQ), BF16)]


def _diff_attn(layer, p, dq, dk, dv, batch, seq):
    blk = pl.BlockSpec((None, seq, ATTN_GROUPS * LANES), lambda b, h, l: (b, 0, h))
    shp = (batch, seq, W_HEADS)
    out = _layer_call(
        _diff_attn_kernel, "diff_attn", (batch, DIFF_HEADS // ATTN_GROUPS),
        in_specs=[_layer_spec(p["lam_tab"].shape), _layer_spec(p["out_gain_rows"].shape), blk, blk, blk],
        out_specs=blk,
        out_shape=jax.ShapeDtypeStruct(shp, BF16),
        scratch_shapes=_attn_scratch(seq),
    )(layer, p["lam_tab"], p["out_gain_rows"], dq.reshape(shp), dk.reshape(shp), dv.reshape(shp))
    return out.reshape(batch * seq, W_HEADS)


def _mla_attn_kernel(q_ref, k_ref, v_ref, o_ref, vt_ref, s_ref, p_ref):
    s_len = q_ref.shape[0]
    tq = TQ
    _store_vt(v_ref, vt_ref)
    row = lax.broadcasted_iota(jnp.int32, (LANES, tq), 0)
    is_head0 = row < MLA_V_DIM

    def job(g, i):
        rows = slice(i * tq, (i + 1) * tq)

        def make_kq():
            heads = [slice((2 * g + hd) * LANES, (2 * g + hd + 1) * LANES) for hd in range(2)]
            return [(heads[hd], q_ref[rows, heads[hd]], slice(hd * tq, (hd + 1) * tq)) for hd in range(2)]

        def finish(acc, l):
            o = acc / l
            o = jnp.where(is_head0, o[:, 0:tq], o[:, tq:2 * tq])
            o_ref[rows, g * LANES:(g + 1) * LANES] = o.T.astype(BF16)

        return i, make_kq, vt_ref.at[g], finish

    jobs = _job_order(job, vt_ref.shape[0], s_len // tq)
    _causal_attention(jobs, tq, k_ref, s_ref, p_ref)


def _mla_attn(mq, mk, mv, batch, seq):
    pairs = ATTN_GROUPS
    qk_blk = pl.BlockSpec((None, seq, pairs * 2 * LANES), lambda b, h: (b, 0, h))
    v_blk = pl.BlockSpec((None, seq, pairs * 2 * MLA_V_DIM), lambda b, h: (b, 0, h))
    qk_shp = (batch, seq, MLA_HEADS * LANES)
    v_shp = (batch, seq, MLA_HEADS * MLA_V_DIM)
    out = pl.pallas_call(
        _mla_attn_kernel,
        grid=(batch, MLA_HEADS // (2 * pairs)),
        in_specs=[qk_blk, qk_blk, v_blk],
        out_specs=v_blk,
        out_shape=jax.ShapeDtypeStruct(v_shp, BF16),
        scratch_shapes=_attn_scratch(seq),
        compiler_params=pltpu.CompilerParams(dimension_semantics=("parallel", "parallel"),
                                             vmem_limit_bytes=VMEM_LIMIT),
        name="mla_attn",
    )(mq.reshape(qk_shp), mk.reshape(qk_shp), mv.reshape(v_shp))
    return out.reshape(batch * seq, MLA_HEADS * MLA_V_DIM)


def _mem_kv_kernel(_layer_ref, mem_ref, mg_ref, w_ref, kg_ref, k_ref, v_ref):
    m = mem_ref[...]
    mn = (m * _inv_rms(m, D_MODEL) * mg_ref[...]).astype(BF16)
    kv = _dot(mn, w_ref[...])
    width = CROSS_HEADS * CROSS_HEAD_DIM
    kg = kg_ref[...]
    for hd in range(CROSS_HEADS):
        sl = slice(hd * LANES, (hd + 1) * LANES)
        kh = kv[:, sl]
        k_ref[:, sl] = (kh * _inv_rms(kh, CROSS_HEAD_DIM) * kg).astype(BF16)
    v_ref[...] = kv[:, width:].astype(BF16)


def _mem_kv(layer, mem2d, p):
    rows = mem2d.shape[0]
    width = CROSS_HEADS * CROSS_HEAD_DIM
    tm = min(512, rows)
    return _layer_call(
        _mem_kv_kernel, "mem_kv", (rows // tm,),
        in_specs=[pl.BlockSpec((tm, D_MODEL), lambda i, l: (i, 0)), _layer_spec(p["mem_gain"].shape),
                  _layer_spec(p["w_mem_kv"].shape), _layer_spec(p["ck_gain"].shape)],
        out_specs=[pl.BlockSpec((tm, width), lambda i, l: (i, 0))] * 2,
        out_shape=[jax.ShapeDtypeStruct((rows, width), BF16)] * 2,
    )(layer, mem2d, p["mem_gain"], p["w_mem_kv"], p["ck_gain"])


FF_CHUNK = 1024


def _post_kernel(_layer_ref, x_ref, od_ref, om_ref, xq_ref, ck_ref, cv_ref, g_ref,
                 wdo_ref, wmo_ref, wco_ref, wout_ref, gmlp_ref, w1_ref, w2_ref, o_ref):
    heads = []
    for hd in range(CROSS_HEADS):
        sl = slice(hd * LANES, (hd + 1) * LANES)
        s = _dot_nt(xq_ref[:, sl], ck_ref[:, sl])
        p = jnp.exp2(s - jnp.max(s, axis=-1, keepdims=True))
        l = jnp.sum(p, axis=-1, keepdims=True)
        heads.append((_dot(p.astype(BF16), cv_ref[:, sl]) / l).astype(BF16))
    oc = jnp.concatenate(heads, axis=1)

    merged = (g_ref[:, 0:D_MODEL].astype(F32) * _dot(od_ref[...], wdo_ref[...])
              + g_ref[:, D_MODEL:2 * D_MODEL].astype(F32) * _dot(om_ref[...], wmo_ref[...])
              + g_ref[:, 2 * D_MODEL:].astype(F32) * _dot(oc, wco_ref[...]))
    x = x_ref[...] + _dot(merged.astype(BF16), wout_ref[...])

    h = (x * _inv_rms(x, D_MODEL) * gmlp_ref[...]).astype(BF16)
    y = x
    for c in range(D_FF // FF_CHUNK):
        sl = slice(c * FF_CHUNK, (c + 1) * FF_CHUNK)
        u = jnp.maximum(_dot(h, w1_ref[:, sl]), 0.0)
        y = y + _dot((u * u).astype(BF16), w2_ref[sl, :])
    o_ref[...] = y


def _post(layer, x, od, om, xq, ck, cv, gates, p, seq):
    t = x.shape[0]
    tm = min(TM_OUT, seq)
    tiles_per_batch = seq // tm
    width = CROSS_HEADS * CROSS_HEAD_DIM
    row = lambda w: pl.BlockSpec((tm, w), lambda i, l: (i, 0))
    mem_blk = pl.BlockSpec((MEM_LEN, width), lambda i, l: (i // tiles_per_batch, 0))
    consts = [p["w_diff_o"], p["w_mla_o"], p["w_cross_o"], p["w_out"], p["g_mlp"], p["w_mlp1"], p["w_mlp2"]]
    return _layer_call(
        _post_kernel, "post", (t // tm,),
        in_specs=[row(D_MODEL), row(od.shape[1]), row(om.shape[1]), row(width), mem_blk, mem_blk,
                  row(gates.shape[1])] + [_layer_spec(c.shape) for c in consts],
        out_specs=row(D_MODEL),
        out_shape=jax.ShapeDtypeStruct((t, D_MODEL), F32),
    )(layer, x, od, om, xq, ck, cv, gates, *consts)


def _pack_params(w_in, b_gate, g_mix, diff_q_gain, diff_k_gain, diff_lambda, diff_out_gain, w_diff_o,
                 mla_qa_gain, w_mla_qb, mla_kva_gain, w_mla_kvb, mla_q_gain, mla_k_gain, w_mla_o,
                 mem_gain, w_mem_kv, cross_q_gain, cross_k_gain, w_cross_o, w_out, g_mlp, w_mlp1, w_mlp2):
    nl = w_in.shape[0]
    d = D_MODEL
    half = DIFF_HEAD_DIM // 2
    o_dq, o_dk, o_dv = 0, 1024, 2048
    o_cq = 3072
    o_ckv = o_cq + MLA_Q_LORA
    o_kr = o_ckv + MLA_KV_LORA
    o_xq = o_kr + MLA_ROPE_DIM

    def diff_cols(w):
        r = w.shape[1]
        return w.reshape(nl, r, DIFF_HEADS, 2, 2, half // 2, 2).transpose(0, 1, 2, 4, 3, 6, 5).reshape(
            nl, r, DIFF_HEADS * LANES)

    def diff_gain(g, scale):
        g = g.reshape(nl, 2, 1, half // 2, 2).transpose(0, 1, 2, 4, 3)
        g = jnp.broadcast_to(g, (nl, 2, 2, 2, half // 2)).reshape(nl, 1, LANES)
        return (g * scale).astype(F32)

    rh = MLA_ROPE_DIM // 2

    def mla_head_layout(nope, rope):
        pad = jnp.zeros(nope.shape[:-1] + (LANES - MLA_QK_DIM,), nope.dtype)
        return jnp.concatenate([rope[..., :rh], nope[..., :HALF_LANES - rh], rope[..., rh:],
                                nope[..., HALF_LANES - rh:], pad], axis=-1)

    kr_w = w_in[:, :, o_kr:o_xq]
    zpad = jnp.zeros((nl, d, HALF_LANES - rh), F32)
    kr_cols = jnp.concatenate([kr_w[..., :rh], zpad, kr_w[..., rh:], zpad], axis=-1)
    w_in_p = jnp.concatenate([diff_cols(w_in[:, :, o_dq:o_dk]).astype(BF16),
                              diff_cols(w_in[:, :, o_dk:o_dv]).astype(BF16),
                              w_in[:, :, o_dv:o_kr].astype(BF16), kr_cols.astype(BF16),
                              w_in[:, :, o_xq:].astype(BF16)], axis=-1)

    wqb = w_mla_qb.reshape(nl, MLA_Q_LORA, MLA_HEADS, MLA_QK_DIM)
    w_qb_p = mla_head_layout(wqb[..., :MLA_NOPE_DIM], wqb[..., MLA_NOPE_DIM:]).reshape(
        nl, MLA_Q_LORA, MLA_HEADS * LANES).astype(BF16)
    wkvb = w_mla_kvb.reshape(nl, MLA_KV_LORA, MLA_HEADS, MLA_NOPE_DIM + MLA_V_DIM)
    k_nope = wkvb[..., :MLA_NOPE_DIM]
    w_kvk_p = mla_head_layout(k_nope, jnp.zeros(k_nope.shape[:-1] + (MLA_ROPE_DIM,), F32)).reshape(
        nl, MLA_KV_LORA, MLA_HEADS * LANES).astype(BF16)
    w_kvv_p = wkvb[..., MLA_NOPE_DIM:].reshape(nl, MLA_KV_LORA, MLA_HEADS * MLA_V_DIM).astype(BF16)

    def mla_gain(g, scale):
        return (mla_head_layout(g[:, :MLA_NOPE_DIM], g[:, MLA_NOPE_DIM:]) * scale)[:, None, :].astype(F32)

    lam_init = jnp.asarray([0.8 - 0.6 * math.exp(-0.3 * l) for l in range(nl)], F32)
    lam_tab = jnp.zeros((nl, SUBLANES, LANES), F32)
    lam_tab = lam_tab.at[:, 0:4, 0:DIFF_HEAD_DIM].set(diff_lambda.astype(F32))
    lam_tab = lam_tab.at[:, 4, :].set(lam_init[:, None])
    out_gain_rows = jnp.broadcast_to(diff_out_gain[:, :, None], (nl, 2 * DIFF_HEAD_DIM, LANES)).astype(F32)

    row = lambda g: g[:, None, :]
    return dict(
        g_mix=row(g_mix), w_in=w_in_p, b_gate=row(b_gate),
        dq_gain=diff_gain(diff_q_gain, DIFF_HEAD_DIM ** -0.5 * LOG2E), dk_gain=diff_gain(diff_k_gain, 1.0),
        qa_gain=row(mla_qa_gain), w_qb=w_qb_p, kva_gain=row(mla_kva_gain),
        w_kvk=w_kvk_p, w_kvv=w_kvv_p,
        mq_gain=mla_gain(mla_q_gain, MLA_QK_DIM ** -0.5 * LOG2E), mk_gain=mla_gain(mla_k_gain, 1.0),
        xq_gain=row(cross_q_gain * (CROSS_HEAD_DIM ** -0.5 * LOG2E)),
        lam_tab=lam_tab, out_gain_rows=out_gain_rows,
        w_diff_o=w_diff_o.astype(BF16), w_mla_o=w_mla_o.astype(BF16),
        mem_gain=row(mem_gain), w_mem_kv=w_mem_kv.astype(BF16), ck_gain=row(cross_k_gain),
        w_cross_o=w_cross_o.astype(BF16), w_out=w_out.astype(BF16),
        g_mlp=row(g_mlp), w_mlp1=w_mlp1.astype(BF16), w_mlp2=w_mlp2.astype(BF16),
    )


def kernel(x, mem, positions, g_mix, w_in, b_gate, diff_q_gain, diff_k_gain, diff_lambda, diff_out_gain,
           w_diff_o, mla_qa_gain, w_mla_qb, mla_kva_gain, w_mla_kvb, mla_q_gain, mla_k_gain, w_mla_o,
           mem_gain, w_mem_kv, cross_q_gain, cross_k_gain, w_cross_o, w_out, g_mlp, w_mlp1, w_mlp2):
    batch, seq, d = x.shape
    tabs = _rope_tables(positions)
    packed = _pack_params(w_in, b_gate, g_mix, diff_q_gain, diff_k_gain, diff_lambda, diff_out_gain,
                          w_diff_o, mla_qa_gain, w_mla_qb, mla_kva_gain, w_mla_kvb, mla_q_gain, mla_k_gain,
                          w_mla_o, mem_gain, w_mem_kv, cross_q_gain, cross_k_gain, w_cross_o, w_out,
                          g_mlp, w_mlp1, w_mlp2)
    xt = x.reshape(batch * seq, d)
    mem2d = mem.reshape(batch * mem.shape[1], d)
    for l in range(w_in.shape[0]):
        layer = jnp.full((1,), l, jnp.int32)
        dq, dk, dv, mq, mk, mv, xq, gates = _inproj(layer, xt, tabs, packed)
        od = _diff_attn(layer, packed, dq, dk, dv, batch, seq)
        om = _mla_attn(mq, mk, mv, batch, seq)
        ck, cv = _mem_kv(layer, mem2d, packed)
        xt = _post(layer, xt, od, om, xq, ck, cv, gates, packed, seq)
    return xt.reshape(batch, seq, d)
```

```python
import math

import jax
import jax.numpy as jnp
from jax import lax
from jax.experimental import pallas as pl
from jax.experimental.pallas import tpu as pltpu

D_MODEL = 1024
DEPTH = 4
MEM_LEN = 256
ROPE_THETA = 10000.0
EPS = 1e-6

DIFF_HEADS = 8
DIFF_HEAD_DIM = 64
MLA_HEADS = 8
MLA_Q_LORA = 384
MLA_KV_LORA = 256
MLA_NOPE_DIM = 64
MLA_ROPE_DIM = 32
MLA_V_DIM = 64
MLA_QK_DIM = MLA_NOPE_DIM + MLA_ROPE_DIM
CROSS_HEADS = 4
CROSS_HEAD_DIM = 128
D_FF = 4 * D_MODEL

LANES = 128
SUBLANES = 8
HALF_LANES = LANES // 2
LOG2E = math.log2(math.e)

W_HEADS = DIFF_HEADS * LANES
C_DQ = 0
C_DK = C_DQ + W_HEADS
C_DV = C_DK + W_HEADS
C_CQ = C_DV + W_HEADS
C_CKV = C_CQ + MLA_Q_LORA
C_KR = C_CKV + MLA_KV_LORA
C_XQ = C_KR + LANES
C_GATE = C_XQ + CROSS_HEADS * CROSS_HEAD_DIM
IN_WIDTH_PACKED = C_GATE + 3 * D_MODEL

TM_IN = 512
TM_OUT = 512
TQ = 256
ATTN_GROUPS = 2
SM_ROWS = 32
VMEM_LIMIT = 56 * 1024 * 1024

BF16 = jnp.bfloat16
F32 = jnp.float32


def _layer_spec(shape):
    nd = len(shape) - 1
    return pl.BlockSpec((None,) + tuple(shape[1:]), lambda *args: (args[-1][0],) + (0,) * nd,
                        pipeline_mode=pl.Buffered(1))


def _layer_call(kernel_fn, name, grid, in_specs, out_specs, out_shape, scratch_shapes=()):
    return pl.pallas_call(
        kernel_fn,
        grid_spec=pltpu.PrefetchScalarGridSpec(num_scalar_prefetch=1, grid=grid, in_specs=in_specs,
                                               out_specs=out_specs, scratch_shapes=scratch_shapes),
        out_shape=out_shape,
        compiler_params=pltpu.CompilerParams(dimension_semantics=("parallel",) * len(grid),
                                             vmem_limit_bytes=VMEM_LIMIT),
        name=name)


def _dot(a, b):
    return jnp.dot(a, b, preferred_element_type=F32)


def _dot_nt(a, b):
    return lax.dot_general(a, b, (((1,), (1,)), ((), ())), preferred_element_type=F32)


def _inv_rms(y, n):
    return lax.rsqrt(jnp.sum(y * y, axis=-1, keepdims=True) * (1.0 / n) + EPS)


def _rope64(y, cos, sin):
    return y * cos + pltpu.roll(y, HALF_LANES, axis=1) * sin


ROPE_FREQS = DIFF_HEAD_DIM // 2
ROPE_GROUPS = LANES // ROPE_FREQS


def _rope_table_kernel(pos_ref, invf_ref, cos_ref, sin_ref):
    ang = pos_ref[...].astype(F32) * invf_ref[...]
    cos, sin = jnp.cos(ang), jnp.sin(ang)
    lane = lax.broadcasted_iota(jnp.int32, ang.shape, 1)
    sign = jnp.where(lane < HALF_LANES, -1.0, 1.0)

    def spread(x, k):
        if k:
            x = pltpu.roll(x, LANES - ROPE_FREQS * k, axis=1)
        x = jnp.where(lane < ROPE_FREQS, x, pltpu.roll(x, ROPE_FREQS, axis=1))
        return jnp.where(lane < HALF_LANES, x, pltpu.roll(x, HALF_LANES, axis=1))

    for k in range(ROPE_GROUPS):
        cos_ref[k] = spread(cos, k)
        sin_ref[k] = spread(sin, k) * sign


def _rope_tables(positions):
    t = positions.size
    rows = t // ROPE_GROUPS
    inv_freq = ROPE_THETA ** (-jnp.arange(0, DIFF_HEAD_DIM, 2, dtype=F32) / DIFF_HEAD_DIM)
    inv_freq = jnp.concatenate([inv_freq[0::2], inv_freq[1::2]])
    pos_rep = jnp.repeat(positions.reshape(ROPE_GROUPS, rows).T, ROPE_FREQS, axis=1)
    invf = jnp.tile(inv_freq, ROPE_GROUPS)[None, :]
    tr = min(1024, rows)
    out_blk = pl.BlockSpec((ROPE_GROUPS, tr, LANES), lambda i: (0, i, 0))
    cos, sin = pl.pallas_call(
        _rope_table_kernel,
        grid=(rows // tr,),
        in_specs=[pl.BlockSpec((tr, LANES), lambda i: (i, 0)),
                  pl.BlockSpec((1, LANES), lambda i: (0, 0))],
        out_specs=[out_blk] * 2,
        out_shape=[jax.ShapeDtypeStruct((ROPE_GROUPS, rows, LANES), F32)] * 2,
        name="rope_tables",
    )(pos_rep, invf)
    return cos.reshape(t, LANES), sin.reshape(t, LANES)


def _inproj_kernel(_layer_ref, x_ref, cosd_ref, sind_ref,
                   gmix_ref, win_ref, bgate_ref, dqg_ref, dkg_ref,
                   qag_ref, wqb_ref, kvag_ref, wkvk_ref, wkvv_ref, mqg_ref, mkg_ref, xqg_ref,
                   dq_ref, dk_ref, dv_ref, mq_ref, mk_ref, mv_ref, xq_ref, gate_ref):
    x = x_ref[...]
    h = (x * _inv_rms(x, D_MODEL) * gmix_ref[...]).astype(BF16)
    tm = x.shape[0]
    rs = slice(None)

    cosd, sind = cosd_ref[...], sind_ref[...]
    lane = lax.broadcasted_iota(jnp.int32, (tm, LANES), 1)
    is_map0 = (lane & 32) == 0
    is_mla_rope = (lane & (HALF_LANES - MLA_ROPE_DIM // 2)) == 0
    cosm = jnp.where(is_mla_rope, cosd, 1.0)
    sinm = jnp.where(is_mla_rope, sind, 0.0)

    def diff_qk(col0, gain_ref, out_ref):
        y = _dot(h, win_ref[:, col0:col0 + W_HEADS])
        gain = gain_ref[...]
        for hd in range(DIFF_HEADS):
            yh = y[:, hd * LANES:(hd + 1) * LANES]
            sq = yh * yh
            s_all = jnp.sum(sq, axis=-1, keepdims=True)
            s0 = jnp.sum(jnp.where(is_map0, sq, 0.0), axis=-1, keepdims=True)
            r0 = lax.rsqrt(s0 * (1.0 / DIFF_HEAD_DIM) + EPS)
            r1 = lax.rsqrt((s_all - s0) * (1.0 / DIFF_HEAD_DIM) + EPS)
            yn = yh * jnp.where(is_map0, r0, r1) * gain
            out_ref[rs, hd * LANES:(hd + 1) * LANES] = _rope64(yn, cosd, sind).astype(BF16)

    def diff_v():
        dv_ref[rs, :] = _dot(h, win_ref[:, C_DV:C_DV + W_HEADS]).astype(BF16)

    def gate(br):
        sl = slice(br * D_MODEL, (br + 1) * D_MODEL)
        z = _dot(h, win_ref[:, C_GATE + br * D_MODEL:C_GATE + (br + 1) * D_MODEL]) + bgate_ref[:, sl]
        gate_ref[rs, sl] = (0.5 * jnp.tanh(0.5 * z) + 0.5).astype(BF16)

    diff_qk(C_DQ, dqg_ref, dq_ref)
    diff_qk(C_DK, dkg_ref, dk_ref)
    diff_v()

    lat = _dot(h, win_ref[:, C_CQ:C_XQ])
    cq = lat[:, 0:MLA_Q_LORA]
    ckv = lat[:, MLA_Q_LORA:MLA_Q_LORA + MLA_KV_LORA]
    kr = lat[:, MLA_Q_LORA + MLA_KV_LORA:]
    cqn = (cq * _inv_rms(cq, MLA_Q_LORA) * qag_ref[...]).astype(BF16)
    ckvn = (ckv * _inv_rms(ckv, MLA_KV_LORA) * kvag_ref[...]).astype(BF16)
    kr = _rope64(kr, cosm, sinm)

    q = _dot(cqn, wqb_ref[...])
    kn = _dot(ckvn, wkvk_ref[...])
    mv_ref[rs, :] = _dot(ckvn, wkvv_ref[...]).astype(BF16)
    mqg, mkg = mqg_ref[...], mkg_ref[...]
    for hd in range(MLA_HEADS):
        sl = slice(hd * LANES, (hd + 1) * LANES)
        qh = _rope64(q[:, sl], cosm, sinm)
        mq_ref[rs, sl] = (qh * _inv_rms(qh, MLA_QK_DIM) * mqg).astype(BF16)
        kh = kn[:, sl] + kr
        mk_ref[rs, sl] = (kh * _inv_rms(kh, MLA_QK_DIM) * mkg).astype(BF16)

    xq = _dot(h, win_ref[:, C_XQ:C_GATE])
    xqg = xqg_ref[...]
    for hd in range(CROSS_HEADS):
        sl = slice(hd * LANES, (hd + 1) * LANES)
        yh = xq[:, sl]
        xq_ref[rs, sl] = (yh * _inv_rms(yh, CROSS_HEAD_DIM) * xqg).astype(BF16)
    for br in range(3):
        gate(br)


def _inproj(layer, x, tabs, p):
    t = x.shape[0]
    tm = min(TM_IN, t)
    row = lambda w: pl.BlockSpec((tm, w), lambda i, l: (i, 0))
    consts = [p["g_mix"], p["w_in"], p["b_gate"], p["dq_gain"], p["dk_gain"],
              p["qa_gain"], p["w_qb"], p["kva_gain"], p["w_kvk"], p["w_kvv"],
              p["mq_gain"], p["mk_gain"], p["xq_gain"]]
    out_widths = [W_HEADS] * 5 + [MLA_HEADS * MLA_V_DIM, CROSS_HEADS * CROSS_HEAD_DIM, 3 * D_MODEL]
    return _layer_call(
        _inproj_kernel, "inproj", (t // tm,),
        in_specs=[row(D_MODEL)] + [row(LANES)] * 2 + [_layer_spec(c.shape) for c in consts],
        out_specs=[row(w) for w in out_widths],
        out_shape=[jax.ShapeDtypeStruct((t, w), BF16) for w in out_widths],
    )(layer, x, *tabs, *consts)


def _causal_bias_t(tq):
    r = lax.broadcasted_iota(jnp.int32, (tq, tq), 0)
    c = lax.broadcasted_iota(jnp.int32, (tq, tq), 1)
    return jnp.where(r <= c, 0.0, -jnp.inf).astype(F32)


def _fold_rows(a, op):
    n, w = a.shape
    return op(op(a.reshape(n // SUBLANES, SUBLANES, w), axis=0), axis=0, keepdims=True)


def _causal_attention(jobs, tq, k_ref, s_ref, p_ref):
    bias_t = _causal_bias_t(tq)
    n_jobs = len(jobs)

    def scores(j):
        i, make_kq, _, _ = jobs[j]
        lo, hi = i * tq, (i + 1) * tq
        s_buf = s_ref.at[j % 2]
        maxima = []
        for k_sl, q_rows, s_sl in make_kq():
            reps = q_rows.shape[0] // tq
            bias = bias_t if reps == 1 else jnp.concatenate([bias_t] * reps, axis=1)
            s = _dot_nt(k_ref[lo:hi, k_sl], q_rows) + bias
            s_buf[lo:hi, s_sl] = s
            m = _fold_rows(s, jnp.max)
            if lo > 0:
                s = _dot_nt(k_ref[0:lo, k_sl], q_rows)
                s_buf[0:lo, s_sl] = s
                m = jnp.maximum(m, _fold_rows(s, jnp.max))
            maxima.append(m)
        return maxima[0] if len(maxima) == 1 else jnp.concatenate(maxima, axis=1)

    def softmax(j, m):
        hi = (jobs[j][0] + 1) * tq
        s_buf, p_buf = s_ref.at[j % 2], p_ref.at[j % 2]
        for r in range(0, hi, SM_ROWS):
            p_buf[r:r + SM_ROWS, :] = jnp.exp2(s_buf[r:r + SM_ROWS, :] - m).astype(BF16)

    def pv(j):
        i, _, vt_ref, finish = jobs[j]
        hi = (i + 1) * tq
        acc = _dot(vt_ref[:, 0:hi], p_ref[j % 2, 0:hi, :])
        finish(acc[0:LANES], acc[LANES:LANES + 1])

    m_next = scores(0)
    for j in range(n_jobs):
        m_cur = m_next
        if j + 1 < n_jobs:
            m_next = scores(j + 1)
        softmax(j, m_cur)
        if j > 0:
            pv(j - 1)
    pv(n_jobs - 1)


def _job_order(job, n_groups, n_tiles):
    return [job(g, i) for i in range(n_tiles) for g in range(n_groups)]


VT_ROWS = LANES + 16


def _store_vt(v_ref, vt_ref):
    for g in range(vt_ref.shape[0]):
        vt_ref[g, 0:LANES, :] = v_ref[:, g * LANES:(g + 1) * LANES].astype(F32).T.astype(BF16)
        vt_ref[g, LANES:VT_ROWS, :] = jnp.ones((VT_ROWS - LANES, vt_ref.shape[2]), BF16)


def _diff_attn_kernel(_layer_ref, lam_ref, og_ref, q_ref, k_ref, v_ref, o_ref, vt_ref, s_ref, p_ref):
    s_len = q_ref.shape[0]
    tq = TQ
    lane = lax.broadcasted_iota(jnp.int32, (tq, LANES), 1)
    is_map0 = (lane & 32) == 0
    _store_vt(v_ref, vt_ref)

    lv = lam_ref[...]
    lam = (jnp.exp(jnp.sum(lv[0:1] * lv[1:2], axis=-1, keepdims=True))
           - jnp.exp(jnp.sum(lv[2:3] * lv[3:4], axis=-1, keepdims=True))
           + lv[4:5, 0:1])
    og = og_ref[...] * (1.0 - lv[4:5, 0:1])
    og = jnp.concatenate([og] * (tq // LANES), axis=1)

    def job(g, i):
        cols = slice(g * LANES, (g + 1) * LANES)
        rows = slice(i * tq, (i + 1) * tq)

        def make_kq():
            q = q_ref[rows, cols]
            zero = jnp.zeros_like(q)
            qq = jnp.concatenate([jnp.where(is_map0, q, zero), jnp.where(is_map0, zero, q)], axis=0)
            return [(cols, qq, slice(None))]

        def finish(acc, l):
            o = acc / l
            o = o[:, 0:tq] - lam * o[:, tq:2 * tq]
            r = lax.rsqrt(jnp.sum(o * o, axis=0, keepdims=True) * (1.0 / (2 * DIFF_HEAD_DIM)) + EPS)
            o_ref[rows, cols] = (o * r * og).T.astype(BF16)

        return i, make_kq, vt_ref.at[g], finish

    jobs = _job_order(job, vt_ref.shape[0], s_len // tq)
    _causal_attention(jobs, tq, k_ref, s_ref, p_ref)


def _attn_scratch(seq):
    return [pltpu.VMEM((ATTN_GROUPS, VT_ROWS, seq), BF16),
            pltpu.VMEM((2, seq, 2 * TQ), F32),
            pltpu.VMEM((2, seq, 2 * TQ), BF16)]


def _diff_attn(layer, p, dq, dk, dv, batch, seq):
    blk = pl.BlockSpec((None, seq, ATTN_GROUPS * LANES), lambda b, h, l: (b, 0, h))
    shp = (batch, seq, W_HEADS)
    out = _layer_call(
        _diff_attn_kernel, "diff_attn", (batch, DIFF_HEADS // ATTN_GROUPS),
        in_specs=[_layer_spec(p["lam_tab"].shape), _layer_spec(p["out_gain_rows"].shape), blk, blk, blk],
        out_specs=blk,
        out_shape=jax.ShapeDtypeStruct(shp, BF16),
        scratch_shapes=_attn_scratch(seq),
    )(layer, p["lam_tab"], p["out_gain_rows"], dq.reshape(shp), dk.reshape(shp), dv.reshape(shp))
    return out.reshape(batch * seq, W_HEADS)


def _mla_attn_kernel(q_ref, k_ref, v_ref, o_ref, vt_ref, s_ref, p_ref):
    s_len = q_ref.shape[0]
    tq = TQ
    _store_vt(v_ref, vt_ref)
    row = lax.broadcasted_iota(jnp.int32, (LANES, tq), 0)
    is_head0 = row < MLA_V_DIM

    def job(g, i):
        rows = slice(i * tq, (i + 1) * tq)

        def make_kq():
            heads = [slice((2 * g + hd) * LANES, (2 * g + hd + 1) * LANES) for hd in range(2)]
            return [(heads[hd], q_ref[rows, heads[hd]], slice(hd * tq, (hd + 1) * tq)) for hd in range(2)]

        def finish(acc, l):
            o = acc / l
            o = jnp.where(is_head0, o[:, 0:tq], o[:, tq:2 * tq])
            o_ref[rows, g * LANES:(g + 1) * LANES] = o.T.astype(BF16)

        return i, make_kq, vt_ref.at[g], finish

    jobs = _job_order(job, vt_ref.shape[0], s_len // tq)
    _causal_attention(jobs, tq, k_ref, s_ref, p_ref)


def _mla_attn(mq, mk, mv, batch, seq):
    pairs = ATTN_GROUPS
    qk_blk = pl.BlockSpec((None, seq, pairs * 2 * LANES), lambda b, h: (b, 0, h))
    v_blk = pl.BlockSpec((None, seq, pairs * 2 * MLA_V_DIM), lambda b, h: (b, 0, h))
    qk_shp = (batch, seq, MLA_HEADS * LANES)
    v_shp = (batch, seq, MLA_HEADS * MLA_V_DIM)
    out = pl.pallas_call(
        _mla_attn_kernel,
        grid=(batch, MLA_HEADS // (2 * pairs)),
        in_specs=[qk_blk, qk_blk, v_blk],
        out_specs=v_blk,
        out_shape=jax.ShapeDtypeStruct(v_shp, BF16),
        scratch_shapes=_attn_scratch(seq),
        compiler_params=pltpu.CompilerParams(dimension_semantics=("parallel", "parallel"),
                                             vmem_limit_bytes=VMEM_LIMIT),
        name="mla_attn",
    )(mq.reshape(qk_shp), mk.reshape(qk_shp), mv.reshape(v_shp))
    return out.reshape(batch * seq, MLA_HEADS * MLA_V_DIM)


def _mem_kv_kernel(_layer_ref, mem_ref, mg_ref, w_ref, kg_ref, k_ref, v_ref):
    m = mem_ref[...]
    mn = (m * _inv_rms(m, D_MODEL) * mg_ref[...]).astype(BF16)
    kv = _dot(mn, w_ref[...])
    width = CROSS_HEADS * CROSS_HEAD_DIM
    kg = kg_ref[...]
    for hd in range(CROSS_HEADS):
        sl = slice(hd * LANES, (hd + 1) * LANES)
        kh = kv[:, sl]
        k_ref[:, sl] = (kh * _inv_rms(kh, CROSS_HEAD_DIM) * kg).astype(BF16)
    v_ref[...] = kv[:, width:].astype(BF16)


def _mem_kv(layer, mem2d, p):
    rows = mem2d.shape[0]
    width = CROSS_HEADS * CROSS_HEAD_DIM
    tm = min(512, rows)
    return _layer_call(
        _mem_kv_kernel, "mem_kv", (rows // tm,),
        in_specs=[pl.BlockSpec((tm, D_MODEL), lambda i, l: (i, 0)), _layer_spec(p["mem_gain"].shape),
                  _layer_spec(p["w_mem_kv"].shape), _layer_spec(p["ck_gain"].shape)],
        out_specs=[pl.BlockSpec((tm, width), lambda i, l: (i, 0))] * 2,
        out_shape=[jax.ShapeDtypeStruct((rows, width), BF16)] * 2,
    )(layer, mem2d, p["mem_gain"], p["w_mem_kv"], p["ck_gain"])


FF_CHUNK = 1024


def _post_kernel(_layer_ref, x_ref, od_ref, om_ref, xq_ref, ck_ref, cv_ref, g_ref,
                 wdo_ref, wmo_ref, wco_ref, wout_ref, gmlp_ref, w1_ref, w2_ref, o_ref):
    heads = []
    for hd in range(CROSS_HEADS):
        sl = slice(hd * LANES, (hd + 1) * LANES)
        s = _dot_nt(xq_ref[:, sl], ck_ref[:, sl])
        p = jnp.exp2(s - jnp.max(s, axis=-1, keepdims=True))
        l = jnp.sum(p, axis=-1, keepdims=True)
        heads.append((_dot(p.astype(BF16), cv_ref[:, sl]) / l).astype(BF16))
    oc = jnp.concatenate(heads, axis=1)

    merged = (g_ref[:, 0:D_MODEL].astype(F32) * _dot(od_ref[...], wdo_ref[...])
              + g_ref[:, D_MODEL:2 * D_MODEL].astype(F32) * _dot(om_ref[...], wmo_ref[...])
              + g_ref[:, 2 * D_MODEL:].astype(F32) * _dot(oc, wco_ref[...]))
    x = x_ref[...] + _dot(merged.astype(BF16), wout_ref[...])

    h = (x * _inv_rms(x, D_MODEL) * gmlp_ref[...]).astype(BF16)
    y = x
    for c in range(D_FF // FF_CHUNK):
        sl = slice(c * FF_CHUNK, (c + 1) * FF_CHUNK)
        u = jnp.maximum(_dot(h, w1_ref[:, sl]), 0.0)
        y = y + _dot((u * u).astype(BF16), w2_ref[sl, :])
    o_ref[...] = y


def _post(layer, x, od, om, xq, ck, cv, gates, p, seq):
    t = x.shape[0]
    tm = min(TM_OUT, seq)
    tiles_per_batch = seq // tm
    width = CROSS_HEADS * CROSS_HEAD_DIM
    row = lambda w: pl.BlockSpec((tm, w), lambda i, l: (i, 0))
    mem_blk = pl.BlockSpec((MEM_LEN, width), lambda i, l: (i // tiles_per_batch, 0))
    consts = [p["w_diff_o"], p["w_mla_o"], p["w_cross_o"], p["w_out"], p["g_mlp"], p["w_mlp1"], p["w_mlp2"]]
    return _layer_call(
        _post_kernel, "post", (t // tm,),
        in_specs=[row(D_MODEL), row(od.shape[1]), row(om.shape[1]), row(width), mem_blk, mem_blk,
                  row(gates.shape[1])] + [_layer_spec(c.shape) for c in consts],
        out_specs=row(D_MODEL),
        out_shape=jax.ShapeDtypeStruct((t, D_MODEL), F32),
    )(layer, x, od, om, xq, ck, cv, gates, *consts)


def _pack_params(w_in, b_gate, g_mix, diff_q_gain, diff_k_gain, diff_lambda, diff_out_gain, w_diff_o,
                 mla_qa_gain, w_mla_qb, mla_kva_gain, w_mla_kvb, mla_q_gain, mla_k_gain, w_mla_o,
                 mem_gain, w_mem_kv, cross_q_gain, cross_k_gain, w_cross_o, w_out, g_mlp, w_mlp1, w_mlp2):
    nl = w_in.shape[0]
    d = D_MODEL
    half = DIFF_HEAD_DIM // 2
    o_dq, o_dk, o_dv = 0, 1024, 2048
    o_cq = 3072
    o_ckv = o_cq + MLA_Q_LORA
    o_kr = o_ckv + MLA_KV_LORA
    o_xq = o_kr + MLA_ROPE_DIM

    def diff_cols(w):
        r = w.shape[1]
        return w.reshape(nl, r, DIFF_HEADS, 2, 2, half // 2, 2).transpose(0, 1, 2, 4, 3, 6, 5).reshape(
            nl, r, DIFF_HEADS * LANES)

    def diff_gain(g, scale):
        g = g.reshape(nl, 2, 1, half // 2, 2).transpose(0, 1, 2, 4, 3)
        g = jnp.broadcast_to(g, (nl, 2, 2, 2, half // 2)).reshape(nl, 1, LANES)
        return (g * scale).astype(F32)

    rh = MLA_ROPE_DIM // 2

    def mla_head_layout(nope, rope):
        pad = jnp.zeros(nope.shape[:-1] + (LANES - MLA_QK_DIM,), nope.dtype)
        return jnp.concatenate([rope[..., :rh], nope[..., :HALF_LANES - rh], rope[..., rh:],
                                nope[..., HALF_LANES - rh:], pad], axis=-1)

    kr_w = w_in[:, :, o_kr:o_xq]
    zpad = jnp.zeros((nl, d, HALF_LANES - rh), F32)
    kr_cols = jnp.concatenate([kr_w[..., :rh], zpad, kr_w[..., rh:], zpad], axis=-1)
    w_in_p = jnp.concatenate([diff_cols(w_in[:, :, o_dq:o_dk]).astype(BF16),
                              diff_cols(w_in[:, :, o_dk:o_dv]).astype(BF16),
                              w_in[:, :, o_dv:o_kr].astype(BF16), kr_cols.astype(BF16),
                              w_in[:, :, o_xq:].astype(BF16)], axis=-1)

    wqb = w_mla_qb.reshape(nl, MLA_Q_LORA, MLA_HEADS, MLA_QK_DIM)
    w_qb_p = mla_head_layout(wqb[..., :MLA_NOPE_DIM], wqb[..., MLA_NOPE_DIM:]).reshape(
        nl, MLA_Q_LORA, MLA_HEADS * LANES).astype(BF16)
    wkvb = w_mla_kvb.reshape(nl, MLA_KV_LORA, MLA_HEADS, MLA_NOPE_DIM + MLA_V_DIM)
    k_nope = wkvb[..., :MLA_NOPE_DIM]
    w_kvk_p = mla_head_layout(k_nope, jnp.zeros(k_nope.shape[:-1] + (MLA_ROPE_DIM,), F32)).reshape(
        nl, MLA_KV_LORA, MLA_HEADS * LANES).astype(BF16)
    w_kvv_p = wkvb[..., MLA_NOPE_DIM:].reshape(nl, MLA_KV_LORA, MLA_HEADS * MLA_V_DIM).astype(BF16)

    def mla_gain(g, scale):
        return (mla_head_layout(g[:, :MLA_NOPE_DIM], g[:, MLA_NOPE_DIM:]) * scale)[:, None, :].astype(F32)

    lam_init = jnp.asarray([0.8 - 0.6 * math.exp(-0.3 * l) for l in range(nl)], F32)
    lam_tab = jnp.zeros((nl, SUBLANES, LANES), F32)
    lam_tab = lam_tab.at[:, 0:4, 0:DIFF_HEAD_DIM].set(diff_lambda.astype(F32))
    lam_tab = lam_tab.at[:, 4, :].set(lam_init[:, None])
    out_gain_rows = jnp.broadcast_to(diff_out_gain[:, :, None], (nl, 2 * DIFF_HEAD_DIM, LANES)).astype(F32)

    row = lambda g: g[:, None, :]
    return dict(
        g_mix=row(g_mix), w_in=w_in_p, b_gate=row(b_gate),
        dq_gain=diff_gain(diff_q_gain, DIFF_HEAD_DIM ** -0.5 * LOG2E), dk_gain=diff_gain(diff_k_gain, 1.0),
        qa_gain=row(mla_qa_gain), w_qb=w_qb_p, kva_gain=row(mla_kva_gain),
        w_kvk=w_kvk_p, w_kvv=w_kvv_p,
        mq_gain=mla_gain(mla_q_gain, MLA_QK_DIM ** -0.5 * LOG2E), mk_gain=mla_gain(mla_k_gain, 1.0),
        xq_gain=row(cross_q_gain * (CROSS_HEAD_DIM ** -0.5 * LOG2E)),
        lam_tab=lam_tab, out_gain_rows=out_gain_rows,
        w_diff_o=w_diff_o.astype(BF16), w_mla_o=w_mla_o.astype(BF16),
        mem_gain=row(mem_gain), w_mem_kv=w_mem_kv.astype(BF16), ck_gain=row(cross_k_gain),
        w_cross_o=w_cross_o.astype(BF16), w_out=w_out.astype(BF16),
        g_mlp=row(g_mlp), w_mlp1=w_mlp1.astype(BF16), w_mlp2=w_mlp2.astype(BF16),
    )


def kernel(x, mem, positions, g_mix, w_in, b_gate, diff_q_gain, diff_k_gain, diff_lambda, diff_out_gain,
           w_diff_o, mla_qa_gain, w_mla_qb, mla_kva_gain, w_mla_kvb, mla_q_gain, mla_k_gain, w_mla_o,
           mem_gain, w_mem_kv, cross_q_gain, cross_k_gain, w_cross_o, w_out, g_mlp, w_mlp1, w_mlp2):
    batch, seq, d = x.shape
    tabs = _rope_tables(positions)
    packed = _pack_params(w_in, b_gate, g_mix, diff_q_gain, diff_k_gain, diff_lambda, diff_out_gain,
                          w_diff_o, mla_qa_gain, w_mla_qb, mla_kva_gain, w_mla_kvb, mla_q_gain, mla_k_gain,
                          w_mla_o, mem_gain, w_mem_kv, cross_q_gain, cross_k_gain, w_cross_o, w_out,
                          g_mlp, w_mlp1, w_mlp2)
    xt = x.reshape(batch * seq, d)
    mem2d = mem.reshape(batch * mem.shape[1], d)
    for l in range(w_in.shape[0]):
        layer = jnp.full((1,), l, jnp.int32)
        dq, dk, dv, mq, mk, mv, xq, gates = _inproj(layer, xt, tabs, packed)
        od = _diff_attn(layer, packed, dq, dk, dv, batch, seq)
        om = _mla_attn(mq, mk, mv, batch, seq)
        ck, cv = _mem_kv(layer, mem2d, packed)
        xt = _post(layer, xt, od, om, xq, ck, cv, gates, packed, seq)
    return xt.reshape(batch, seq, d)
```

```python
import math

import jax
import jax.numpy as jnp
from jax import lax
from jax.experimental import pallas as pl
from jax.experimental.pallas import tpu as pltpu

D_MODEL = 1024
DEPTH = 4
MEM_LEN = 256
ROPE_THETA = 10000.0
EPS = 1e-6

DIFF_HEADS = 8
DIFF_HEAD_DIM = 64
MLA_HEADS = 8
MLA_Q_LORA = 384
MLA_KV_LORA = 256
MLA_NOPE_DIM = 64
MLA_ROPE_DIM = 32
MLA_V_DIM = 64
MLA_QK_DIM = MLA_NOPE_DIM + MLA_ROPE_DIM
CROSS_HEADS = 4
CROSS_HEAD_DIM = 128
D_FF = 4 * D_MODEL

LANES = 128
SUBLANES = 8
HALF_LANES = LANES // 2
LOG2E = math.log2(math.e)

W_HEADS = DIFF_HEADS * LANES
C_DQ = 0
C_DK = C_DQ + W_HEADS
C_DV = C_DK + W_HEADS
C_CQ = C_DV + W_HEADS
C_CKV = C_CQ + MLA_Q_LORA
C_KR = C_CKV + MLA_KV_LORA
C_XQ = C_KR + LANES
C_GATE = C_XQ + CROSS_HEADS * CROSS_HEAD_DIM
IN_WIDTH_PACKED = C_GATE + 3 * D_MODEL

TM_IN = 512
TM_OUT = 512
TQ = 256
ATTN_GROUPS = 2
SM_ROWS = 32
VMEM_LIMIT = 56 * 1024 * 1024

BF16 = jnp.bfloat16
F32 = jnp.float32


def _layer_spec(shape):
    nd = len(shape) - 1
    return pl.BlockSpec((None,) + tuple(shape[1:]), lambda *args: (args[-1][0],) + (0,) * nd,
                        pipeline_mode=pl.Buffered(1))


def _layer_call(kernel_fn, name, grid, in_specs, out_specs, out_shape, scratch_shapes=()):
    return pl.pallas_call(
        kernel_fn,
        grid_spec=pltpu.PrefetchScalarGridSpec(num_scalar_prefetch=1, grid=grid, in_specs=in_specs,
                                               out_specs=out_specs, scratch_shapes=scratch_shapes),
        out_shape=out_shape,
        compiler_params=pltpu.CompilerParams(dimension_semantics=("parallel",) * len(grid),
                                             vmem_limit_bytes=VMEM_LIMIT),
        name=name)


def _dot(a, b):
    return jnp.dot(a, b, preferred_element_type=F32)


def _dot_nt(a, b):
    return lax.dot_general(a, b, (((1,), (1,)), ((), ())), preferred_element_type=F32)


def _inv_rms(y, n):
    return lax.rsqrt(jnp.sum(y * y, axis=-1, keepdims=True) * (1.0 / n) + EPS)


def _rope64(y, cos, sin):
    return y * cos + pltpu.roll(y, HALF_LANES, axis=1) * sin


ROPE_FREQS = DIFF_HEAD_DIM // 2
ROPE_GROUPS = LANES // ROPE_FREQS


def _rope_table_kernel(pos_ref, invf_ref, cos_ref, sin_ref):
    ang = pos_ref[...].astype(F32) * invf_ref[...]
    cos, sin = jnp.cos(ang), jnp.sin(ang)
    lane = lax.broadcasted_iota(jnp.int32, ang.shape, 1)
    sign = jnp.where(lane < HALF_LANES, -1.0, 1.0)

    def spread(x, k):
        if k:
            x = pltpu.roll(x, LANES - ROPE_FREQS * k, axis=1)
        x = jnp.where(lane < ROPE_FREQS, x, pltpu.roll(x, ROPE_FREQS, axis=1))
        return jnp.where(lane < HALF_LANES, x, pltpu.roll(x, HALF_LANES, axis=1))

    for k in range(ROPE_GROUPS):
        cos_ref[k] = spread(cos, k)
        sin_ref[k] = spread(sin, k) * sign


def _rope_tables(positions):
    t = positions.size
    rows = t // ROPE_GROUPS
    inv_freq = ROPE_THETA ** (-jnp.arange(0, DIFF_HEAD_DIM, 2, dtype=F32) / DIFF_HEAD_DIM)
    inv_freq = jnp.concatenate([inv_freq[0::2], inv_freq[1::2]])
    pos_rep = jnp.repeat(positions.reshape(ROPE_GROUPS, rows).T, ROPE_FREQS, axis=1)
    invf = jnp.tile(inv_freq, ROPE_GROUPS)[None, :]
    tr = min(1024, rows)
    out_blk = pl.BlockSpec((ROPE_GROUPS, tr, LANES), lambda i: (0, i, 0))
    cos, sin = pl.pallas_call(
        _rope_table_kernel,
        grid=(rows // tr,),
        in_specs=[pl.BlockSpec((tr, LANES), lambda i: (i, 0)),
                  pl.BlockSpec((1, LANES), lambda i: (0, 0))],
        out_specs=[out_blk] * 2,
        out_shape=[jax.ShapeDtypeStruct((ROPE_GROUPS, rows, LANES), F32)] * 2,
        name="rope_tables",
    )(pos_rep, invf)
    return cos.reshape(t, LANES), sin.reshape(t, LANES)


def _inproj_kernel(_layer_ref, x_ref, cosd_ref, sind_ref,
                   gmix_ref, win_ref, bgate_ref, dqg_ref, dkg_ref,
                   qag_ref, wqb_ref, kvag_ref, wkvk_ref, wkvv_ref, mqg_ref, mkg_ref, xqg_ref,
                   dq_ref, dk_ref, dv_ref, mq_ref, mk_ref, mv_ref, xq_ref, gate_ref):
    x = x_ref[...]
    h = (x * _inv_rms(x, D_MODEL) * gmix_ref[...]).astype(BF16)
    tm = x.shape[0]
    rs = slice(None)

    cosd, sind = cosd_ref[...], sind_ref[...]
    lane = lax.broadcasted_iota(jnp.int32, (tm, LANES), 1)
    is_map0 = (lane & 32) == 0
    is_mla_rope = (lane & (HALF_LANES - MLA_ROPE_DIM // 2)) == 0
    cosm = jnp.where(is_mla_rope, cosd, 1.0)
    sinm = jnp.where(is_mla_rope, sind, 0.0)

    def diff_qk(col0, gain_ref, out_ref):
        y = _dot(h, win_ref[:, col0:col0 + W_HEADS])
        gain = gain_ref[...]
        for hd in range(DIFF_HEADS):
            yh = y[:, hd * LANES:(hd + 1) * LANES]
            sq = yh * yh
            s_all = jnp.sum(sq, axis=-1, keepdims=True)
            s0 = jnp.sum(jnp.where(is_map0, sq, 0.0), axis=-1, keepdims=True)
            r = lax.rsqrt(jnp.where(is_map0, s0, s_all - s0) * (1.0 / DIFF_HEAD_DIM) + EPS)
            yn = yh * r * gain
            out_ref[rs, hd * LANES:(hd + 1) * LANES] = _rope64(yn, cosd, sind).astype(BF16)

    def diff_v():
        dv_ref[rs, :] = _dot(h, win_ref[:, C_DV:C_DV + W_HEADS]).astype(BF16)

    def gate(br):
        sl = slice(br * D_MODEL, (br + 1) * D_MODEL)
        z = _dot(h, win_ref[:, C_GATE + br * D_MODEL:C_GATE + (br + 1) * D_MODEL]) + bgate_ref[:, sl]
        gate_ref[rs, sl] = (0.5 * jnp.tanh(0.5 * z) + 0.5).astype(BF16)

    diff_qk(C_DQ, dqg_ref, dq_ref)
    diff_qk(C_DK, dkg_ref, dk_ref)
    diff_v()

    lat = _dot(h, win_ref[:, C_CQ:C_XQ])
    cq = lat[:, 0:MLA_Q_LORA]
    ckv = lat[:, MLA_Q_LORA:MLA_Q_LORA + MLA_KV_LORA]
    kr = lat[:, MLA_Q_LORA + MLA_KV_LORA:]
    cqn = (cq * _inv_rms(cq, MLA_Q_LORA) * qag_ref[...]).astype(BF16)
    ckvn = (ckv * _inv_rms(ckv, MLA_KV_LORA) * kvag_ref[...]).astype(BF16)
    kr = _rope64(kr, cosm, sinm)

    q = _dot(cqn, wqb_ref[...])
    kn = _dot(ckvn, wkvk_ref[...])
    mv_ref[rs, :] = _dot(ckvn, wkvv_ref[...]).astype(BF16)
    mqg, mkg = mqg_ref[...], mkg_ref[...]
    for hd in range(MLA_HEADS):
        sl = slice(hd * LANES, (hd + 1) * LANES)
        qh = _rope64(q[:, sl], cosm, sinm)
        mq_ref[rs, sl] = (qh * _inv_rms(qh, MLA_QK_DIM) * mqg).astype(BF16)
        kh = kn[:, sl] + kr
        mk_ref[rs, sl] = (kh * _inv_rms(kh, MLA_QK_DIM) * mkg).astype(BF16)

    xq = _dot(h, win_ref[:, C_XQ:C_GATE])
    xqg = xqg_ref[...]
    for hd in range(CROSS_HEADS):
        sl = slice(hd * LANES, (hd + 1) * LANES)
        yh = xq[:, sl]
        xq_ref[rs, sl] = (yh * _inv_rms(yh, CROSS_HEAD_DIM) * xqg).astype(BF16)
    for br in range(3):
        gate(br)


def _inproj(layer, x, tabs, p):
    t = x.shape[0]
    tm = min(TM_IN, t)
    row = lambda w: pl.BlockSpec((tm, w), lambda i, l: (i, 0))
    consts = [p["g_mix"], p["w_in"], p["b_gate"], p["dq_gain"], p["dk_gain"],
              p["qa_gain"], p["w_qb"], p["kva_gain"], p["w_kvk"], p["w_kvv"],
              p["mq_gain"], p["mk_gain"], p["xq_gain"]]
    out_widths = [W_HEADS] * 5 + [MLA_HEADS * MLA_V_DIM, CROSS_HEADS * CROSS_HEAD_DIM, 3 * D_MODEL]
    return _layer_call(
        _inproj_kernel, "inproj", (t // tm,),
        in_specs=[row(D_MODEL)] + [row(LANES)] * 2 + [_layer_spec(c.shape) for c in consts],
        out_specs=[row(w) for w in out_widths],
        out_shape=[jax.ShapeDtypeStruct((t, w), BF16) for w in out_widths],
    )(layer, x, *tabs, *consts)


def _causal_bias_t(tq):
    r = lax.broadcasted_iota(jnp.int32, (tq, tq), 0)
    c = lax.broadcasted_iota(jnp.int32, (tq, tq), 1)
    return jnp.where(r <= c, 0.0, -jnp.inf).astype(F32)


def _fold_rows(a, op):
    n, w = a.shape
    return op(op(a.reshape(n // SUBLANES, SUBLANES, w), axis=0), axis=0, keepdims=True)


def _causal_attention(jobs, tq, k_ref, s_ref, p_ref):
    bias_t = _causal_bias_t(tq)
    n_jobs = len(jobs)

    def scores(j):
        i, make_kq, _, _ = jobs[j]
        lo, hi = i * tq, (i + 1) * tq
        s_buf = s_ref.at[j % 2]
        maxima = []
        for k_sl, q_rows, s_sl in make_kq():
            reps = q_rows.shape[0] // tq
            bias = bias_t if reps == 1 else jnp.concatenate([bias_t] * reps, axis=1)
            s = _dot_nt(k_ref[lo:hi, k_sl], q_rows) + bias
            s_buf[lo:hi, s_sl] = s
            m = _fold_rows(s, jnp.max)
            if lo > 0:
                s = _dot_nt(k_ref[0:lo, k_sl], q_rows)
                s_buf[0:lo, s_sl] = s
                m = jnp.maximum(m, _fold_rows(s, jnp.max))
            maxima.append(m)
        return maxima[0] if len(maxima) == 1 else jnp.concatenate(maxima, axis=1)

    def softmax(j, m):
        hi = (jobs[j][0] + 1) * tq
        s_buf, p_buf = s_ref.at[j % 2], p_ref.at[j % 2]
        for r in range(0, hi, SM_ROWS):
            p_buf[r:r + SM_ROWS, :] = jnp.exp2(s_buf[r:r + SM_ROWS, :] - m).astype(BF16)

    def pv(j):
        i, _, vt_ref, finish = jobs[j]
        hi = (i + 1) * tq
        acc = _dot(vt_ref[:, 0:hi], p_ref[j % 2, 0:hi, :])
        finish(acc[0:LANES], acc[LANES:LANES + 1])

    m_next = scores(0)
    for j in range(n_jobs):
        m_cur = m_next
        if j + 1 < n_jobs:
            m_next = scores(j + 1)
        softmax(j, m_cur)
        if j > 0:
            pv(j - 1)
    pv(n_jobs - 1)


def _job_order(job, n_groups, n_tiles):
    return [job(g, i) for i in range(n_tiles) for g in range(n_groups)]


VT_ROWS = LANES + 16


def _store_vt(v_ref, vt_ref):
    for g in range(vt_ref.shape[0]):
        vt_ref[g, 0:LANES, :] = v_ref[:, g * LANES:(g + 1) * LANES].astype(F32).T.astype(BF16)
        vt_ref[g, LANES:VT_ROWS, :] = jnp.ones((VT_ROWS - LANES, vt_ref.shape[2]), BF16)


def _diff_attn_kernel(_layer_ref, lam_ref, og_ref, q_ref, k_ref, v_ref, o_ref, vt_ref, s_ref, p_ref):
    s_len = q_ref.shape[0]
    tq = TQ
    lane = lax.broadcasted_iota(jnp.int32, (tq, LANES), 1)
    is_map0 = (lane & 32) == 0
    _store_vt(v_ref, vt_ref)

    lv = lam_ref[...]
    lam = (jnp.exp(jnp.sum(lv[0:1] * lv[1:2], axis=-1, keepdims=True))
           - jnp.exp(jnp.sum(lv[2:3] * lv[3:4], axis=-1, keepdims=True))
           + lv[4:5, 0:1])
    og = og_ref[...] * (1.0 - lv[4:5, 0:1])
    og = jnp.concatenate([og] * (tq // LANES), axis=1)

    def job(g, i):
        cols = slice(g * LANES, (g + 1) * LANES)
        rows = slice(i * tq, (i + 1) * tq)

        def make_kq():
            q = q_ref[rows, cols]
            zero = jnp.zeros_like(q)
            qq = jnp.concatenate([jnp.where(is_map0, q, zero), jnp.where(is_map0, zero, q)], axis=0)
            return [(cols, qq, slice(None))]

        def finish(acc, l):
            o = acc / l
            o = o[:, 0:tq] - lam * o[:, tq:2 * tq]
            r = lax.rsqrt(jnp.sum(o * o, axis=0, keepdims=True) * (1.0 / (2 * DIFF_HEAD_DIM)) + EPS)
            o_ref[rows, cols] = (o * r * og).T.astype(BF16)

        return i, make_kq, vt_ref.at[g], finish

    jobs = _job_order(job, vt_ref.shape[0], s_len // tq)
    _causal_attention(jobs, tq, k_ref, s_ref, p_ref)


def _attn_scratch(seq):
    return [pltpu.VMEM((ATTN_GROUPS, VT_ROWS, seq), BF16),
            pltpu.VMEM((2, seq, 2 * TQ), F32),
            pltpu.VMEM((2, seq, 2 * TQ), BF16)]


def _diff_attn(layer, p, dq, dk, dv, batch, seq):
    blk = pl.BlockSpec((None, seq, ATTN_GROUPS * LANES), lambda b, h, l: (b, 0, h))
    shp = (batch, seq, W_HEADS)
    out = _layer_call(
        _diff_attn_kernel, "diff_attn", (batch, DIFF_HEADS // ATTN_GROUPS),
        in_specs=[_layer_spec(p["lam_tab"].shape), _layer_spec(p["out_gain_rows"].shape), blk, blk, blk],
        out_specs=blk,
        out_shape=jax.ShapeDtypeStruct(shp, BF16),
        scratch_shapes=_attn_scratch(seq),
    )(layer, p["lam_tab"], p["out_gain_rows"], dq.reshape(shp), dk.reshape(shp), dv.reshape(shp))
    return out.reshape(batch * seq, W_HEADS)


def _mla_attn_kernel(q_ref, k_ref, v_ref, o_ref, vt_ref, s_ref, p_ref):
    s_len = q_ref.shape[0]
    tq = TQ
    _store_vt(v_ref, vt_ref)
    row = lax.broadcasted_iota(jnp.int32, (LANES, tq), 0)
    is_head0 = row < MLA_V_DIM

    def job(g, i):
        rows = slice(i * tq, (i + 1) * tq)

        def make_kq():
            heads = [slice((2 * g + hd) * LANES, (2 * g + hd + 1) * LANES) for hd in range(2)]
            return [(heads[hd], q_ref[rows, heads[hd]], slice(hd * tq, (hd + 1) * tq)) for hd in range(2)]

        def finish(acc, l):
            o = acc / l
            o = jnp.where(is_head0, o[:, 0:tq], o[:, tq:2 * tq])
            o_ref[rows, g * LANES:(g + 1) * LANES] = o.T.astype(BF16)

        return i, make_kq, vt_ref.at[g], finish

    jobs = _job_order(job, vt_ref.shape[0], s_len // tq)
    _causal_attention(jobs, tq, k_ref, s_ref, p_ref)


def _mla_attn(mq, mk, mv, batch, seq):
    pairs = ATTN_GROUPS
    qk_blk = pl.BlockSpec((None, seq, pairs * 2 * LANES), lambda b, h: (b, 0, h))
    v_blk = pl.BlockSpec((None, seq, pairs * 2 * MLA_V_DIM), lambda b, h: (b, 0, h))
    qk_shp = (batch, seq, MLA_HEADS * LANES)
    v_shp = (batch, seq, MLA_HEADS * MLA_V_DIM)
    out = pl.pallas_call(
        _mla_attn_kernel,
        grid=(batch, MLA_HEADS // (2 * pairs)),
        in_specs=[qk_blk, qk_blk, v_blk],
        out_specs=v_blk,
        out_shape=jax.ShapeDtypeStruct(v_shp, BF16),
        scratch_shapes=_attn_scratch(seq),
        compiler_params=pltpu.CompilerParams(dimension_semantics=("parallel", "parallel"),
                                             vmem_limit_bytes=VMEM_LIMIT),
        name="mla_attn",
    )(mq.reshape(qk_shp), mk.reshape(qk_shp), mv.reshape(v_shp))
    return out.reshape(batch * seq, MLA_HEADS * MLA_V_DIM)


def _mem_kv_kernel(_layer_ref, mem_ref, mg_ref, w_ref, kg_ref, k_ref, v_ref):
    m = mem_ref[...]
    mn = (m * _inv_rms(m, D_MODEL) * mg_ref[...]).astype(BF16)
    kv = _dot(mn, w_ref[...])
    width = CROSS_HEADS * CROSS_HEAD_DIM
    kg = kg_ref[...]
    for hd in range(CROSS_HEADS):
        sl = slice(hd * LANES, (hd + 1) * LANES)
        kh = kv[:, sl]
        k_ref[:, sl] = (kh * _inv_rms(kh, CROSS_HEAD_DIM) * kg).astype(BF16)
    v_ref[...] = kv[:, width:].astype(BF16)


def _mem_kv(layer, mem2d, p):
    rows = mem2d.shape[0]
    width = CROSS_HEADS * CROSS_HEAD_DIM
    tm = min(512, rows)
    return _layer_call(
        _mem_kv_kernel, "mem_kv", (rows // tm,),
        in_specs=[pl.BlockSpec((tm, D_MODEL), lambda i, l: (i, 0)), _layer_spec(p["mem_gain"].shape),
                  _layer_spec(p["w_mem_kv"].shape), _layer_spec(p["ck_gain"].shape)],
        out_specs=[pl.BlockSpec((tm, width), lambda i, l: (i, 0))] * 2,
        out_shape=[jax.ShapeDtypeStruct((rows, width), BF16)] * 2,
    )(layer, mem2d, p["mem_gain"], p["w_mem_kv"], p["ck_gain"])


FF_CHUNK = 1024


def _post_kernel(_layer_ref, x_ref, od_ref, om_ref, xq_ref, ck_ref, cv_ref, g_ref,
                 wdo_ref, wmo_ref, wco_ref, wout_ref, gmlp_ref, w1_ref, w2_ref, o_ref):
    heads = []
    for hd in range(CROSS_HEADS):
        sl = slice(hd * LANES, (hd + 1) * LANES)
        s = _dot_nt(xq_ref[:, sl], ck_ref[:, sl])
        p = jnp.exp2(s - jnp.max(s, axis=-1, keepdims=True))
        l = jnp.sum(p, axis=-1, keepdims=True)
        heads.append((_dot(p.astype(BF16), cv_ref[:, sl]) / l).astype(BF16))
    oc = jnp.concatenate(heads, axis=1)

    merged = (g_ref[:, 0:D_MODEL].astype(F32) * _dot(od_ref[...], wdo_ref[...])
              + g_ref[:, D_MODEL:2 * D_MODEL].astype(F32) * _dot(om_ref[...], wmo_ref[...])
              + g_ref[:, 2 * D_MODEL:].astype(F32) * _dot(oc, wco_ref[...]))
    x = x_ref[...] + _dot(merged.astype(BF16), wout_ref[...])

    h = (x * _inv_rms(x, D_MODEL) * gmlp_ref[...]).astype(BF16)
    y = x
    for c in range(D_FF // FF_CHUNK):
        sl = slice(c * FF_CHUNK, (c + 1) * FF_CHUNK)
        u = jnp.maximum(_dot(h, w1_ref[:, sl]), 0.0)
        y = y + _dot((u * u).astype(BF16), w2_ref[sl, :])
    o_ref[...] = y


def _post(layer, x, od, om, xq, ck, cv, gates, p, seq):
    t = x.shape[0]
    tm = min(TM_OUT, seq)
    tiles_per_batch = seq // tm
    width = CROSS_HEADS * CROSS_HEAD_DIM
    row = lambda w: pl.BlockSpec((tm, w), lambda i, l: (i, 0))
    mem_blk = pl.BlockSpec((MEM_LEN, width), lambda i, l: (i // tiles_per_batch, 0))
    consts = [p["w_diff_o"], p["w_mla_o"], p["w_cross_o"], p["w_out"], p["g_mlp"], p["w_mlp1"], p["w_mlp2"]]
    return _layer_call(
        _post_kernel, "post", (t // tm,),
        in_specs=[row(D_MODEL), row(od.shape[1]), row(om.shape[1]), row(width), mem_blk, mem_blk,
                  row(gates.shape[1])] + [_layer_spec(c.shape) for c in consts],
        out_specs=row(D_MODEL),
        out_shape=jax.ShapeDtypeStruct((t, D_MODEL), F32),
    )(layer, x, od, om, xq, ck, cv, gates, *consts)


def _pack_params(w_in, b_gate, g_mix, diff_q_gain, diff_k_gain, diff_lambda, diff_out_gain, w_diff_o,
                 mla_qa_gain, w_mla_qb, mla_kva_gain, w_mla_kvb, mla_q_gain, mla_k_gain, w_mla_o,
                 mem_gain, w_mem_kv, cross_q_gain, cross_k_gain, w_cross_o, w_out, g_mlp, w_mlp1, w_mlp2):
    nl = w_in.shape[0]
    d = D_MODEL
    half = DIFF_HEAD_DIM // 2
    o_dq, o_dk, o_dv = 0, 1024, 2048
    o_cq = 3072
    o_ckv = o_cq + MLA_Q_LORA
    o_kr = o_ckv + MLA_KV_LORA
    o_xq = o_kr + MLA_ROPE_DIM

    def diff_cols(w):
        r = w.shape[1]
        return w.reshape(nl, r, DIFF_HEADS, 2, 2, half // 2, 2).transpose(0, 1, 2, 4, 3, 6, 5).reshape(
            nl, r, DIFF_HEADS * LANES)

    def diff_gain(g, scale):
        g = g.reshape(nl, 2, 1, half // 2, 2).transpose(0, 1, 2, 4, 3)
        g = jnp.broadcast_to(g, (nl, 2, 2, 2, half // 2)).reshape(nl, 1, LANES)
        return (g * scale).astype(F32)

    rh = MLA_ROPE_DIM // 2

    def mla_head_layout(nope, rope):
        pad = jnp.zeros(nope.shape[:-1] + (LANES - MLA_QK_DIM,), nope.dtype)
        return jnp.concatenate([rope[..., :rh], nope[..., :HALF_LANES - rh], rope[..., rh:],
                                nope[..., HALF_LANES - rh:], pad], axis=-1)

    kr_w = w_in[:, :, o_kr:o_xq]
    zpad = jnp.zeros((nl, d, HALF_LANES - rh), F32)
    kr_cols = jnp.concatenate([kr_w[..., :rh], zpad, kr_w[..., rh:], zpad], axis=-1)
    w_in_p = jnp.concatenate([diff_cols(w_in[:, :, o_dq:o_dk]).astype(BF16),
                              diff_cols(w_in[:, :, o_dk:o_dv]).astype(BF16),
                              w_in[:, :, o_dv:o_kr].astype(BF16), kr_cols.astype(BF16),
                              w_in[:, :, o_xq:].astype(BF16)], axis=-1)

    wqb = w_mla_qb.reshape(nl, MLA_Q_LORA, MLA_HEADS, MLA_QK_DIM)
    w_qb_p = mla_head_layout(wqb[..., :MLA_NOPE_DIM], wqb[..., MLA_NOPE_DIM:]).reshape(
        nl, MLA_Q_LORA, MLA_HEADS * LANES).astype(BF16)
    wkvb = w_mla_kvb.reshape(nl, MLA_KV_LORA, MLA_HEADS, MLA_NOPE_DIM + MLA_V_DIM)
    k_nope = wkvb[..., :MLA_NOPE_DIM]
    w_kvk_p = mla_head_layout(k_nope, jnp.zeros(k_nope.shape[:-1] + (MLA_ROPE_DIM,), F32)).reshape(
        nl, MLA_KV_LORA, MLA_HEADS * LANES).astype(BF16)
    w_kvv_p = wkvb[..., MLA_NOPE_DIM:].reshape(nl, MLA_KV_LORA, MLA_HEADS * MLA_V_DIM).astype(BF16)

    def mla_gain(g, scale):
        return (mla_head_layout(g[:, :MLA_NOPE_DIM], g[:, MLA_NOPE_DIM:]) * scale)[:, None, :].astype(F32)

    lam_init = jnp.asarray([0.8 - 0.6 * math.exp(-0.3 * l) for l in range(nl)], F32)
    lam_tab = jnp.zeros((nl, SUBLANES, LANES), F32)
    lam_tab = lam_tab.at[:, 0:4, 0:DIFF_HEAD_DIM].set(diff_lambda.astype(F32))
    lam_tab = lam_tab.at[:, 4, :].set(lam_init[:, None])
    out_gain_rows = jnp.broadcast_to(diff_out_gain[:, :, None], (nl, 2 * DIFF_HEAD_DIM, LANES)).astype(F32)

    row = lambda g: g[:, None, :]
    return dict(
        g_mix=row(g_mix), w_in=w_in_p, b_gate=row(b_gate),
        dq_gain=diff_gain(diff_q_gain, DIFF_HEAD_DIM ** -0.5 * LOG2E), dk_gain=diff_gain(diff_k_gain, 1.0),
        qa_gain=row(mla_qa_gain), w_qb=w_qb_p, kva_gain=row(mla_kva_gain),
        w_kvk=w_kvk_p, w_kvv=w_kvv_p,
        mq_gain=mla_gain(mla_q_gain, MLA_QK_DIM ** -0.5 * LOG2E), mk_gain=mla_gain(mla_k_gain, 1.0),
        xq_gain=row(cross_q_gain * (CROSS_HEAD_DIM ** -0.5 * LOG2E)),
        lam_tab=lam_tab, out_gain_rows=out_gain_rows,
        w_diff_o=w_diff_o.astype(BF16), w_mla_o=w_mla_o.astype(BF16),
        mem_gain=row(mem_gain), w_mem_kv=w_mem_kv.astype(BF16), ck_gain=row(cross_k_gain),
        w_cross_o=w_cross_o.astype(BF16), w_out=w_out.astype(BF16),
        g_mlp=row(g_mlp), w_mlp1=w_mlp1.astype(BF16), w_mlp2=w_mlp2.astype(BF16),
    )


def kernel(x, mem, positions, g_mix, w_in, b_gate, diff_q_gain, diff_k_gain, diff_lambda, diff_out_gain,
           w_diff_o, mla_qa_gain, w_mla_qb, mla_kva_gain, w_mla_kvb, mla_q_gain, mla_k_gain, w_mla_o,
           mem_gain, w_mem_kv, cross_q_gain, cross_k_gain, w_cross_o, w_out, g_mlp, w_mlp1, w_mlp2):
    batch, seq, d = x.shape
    tabs = _rope_tables(positions)
    packed = _pack_params(w_in, b_gate, g_mix, diff_q_gain, diff_k_gain, diff_lambda, diff_out_gain,
                          w_diff_o, mla_qa_gain, w_mla_qb, mla_kva_gain, w_mla_kvb, mla_q_gain, mla_k_gain,
                          w_mla_o, mem_gain, w_mem_kv, cross_q_gain, cross_k_gain, w_cross_o, w_out,
                          g_mlp, w_mlp1, w_mlp2)
    xt = x.reshape(batch * seq, d)
    mem2d = mem.reshape(batch * mem.shape[1], d)
    for l in range(w_in.shape[0]):
        layer = jnp.full((1,), l, jnp.int32)
        dq, dk, dv, mq, mk, mv, xq, gates = _inproj(layer, xt, tabs, packed)
        od = _diff_attn(layer, packed, dq, dk, dv, batch, seq)
        om = _mla_attn(mq, mk, mv, batch, seq)
        ck, cv = _mem_kv(layer, mem2d, packed)
        xt = _post(layer, xt, od, om, xq, ck, cv, gates, packed, seq)
    return xt.reshape(batch, seq, d)
```

```python
import math

import jax
import jax.numpy as jnp
from jax import lax
from jax.experimental import pallas as pl
from jax.experimental.pallas import tpu as pltpu

D_MODEL = 1024
DEPTH = 4
MEM_LEN = 256
ROPE_THETA = 10000.0
EPS = 1e-6

DIFF_HEADS = 8
DIFF_HEAD_DIM = 64
MLA_HEADS = 8
MLA_Q_LORA = 384
MLA_KV_LORA = 256
MLA_NOPE_DIM = 64
MLA_ROPE_DIM = 32
MLA_V_DIM = 64
MLA_QK_DIM = MLA_NOPE_DIM + MLA_ROPE_DIM
CROSS_HEADS = 4
CROSS_HEAD_DIM = 128
D_FF = 4 * D_MODEL

LANES = 128
SUBLANES = 8
HALF_LANES = LANES // 2
LOG2E = math.log2(math.e)

W_HEADS = DIFF_HEADS * LANES
C_DQ = 0
C_DK = C_DQ + W_HEADS
C_DV = C_DK + W_HEADS
C_CQ = C_DV + W_HEADS
C_CKV = C_CQ + MLA_Q_LORA
C_KR = C_CKV + MLA_KV_LORA
C_XQ = C_KR + LANES
C_GATE = C_XQ + CROSS_HEADS * CROSS_HEAD_DIM
IN_WIDTH_PACKED = C_GATE + 3 * D_MODEL

TM_IN = 512
TM_OUT = 512
TQ = 256
ATTN_GROUPS = 2
SM_ROWS = 32
VMEM_LIMIT = 56 * 1024 * 1024

BF16 = jnp.bfloat16
F32 = jnp.float32


def _layer_spec(shape):
    nd = len(shape) - 1
    return pl.BlockSpec((None,) + tuple(shape[1:]), lambda *args: (args[-1][0],) + (0,) * nd,
                        pipeline_mode=pl.Buffered(1))


def _layer_call(kernel_fn, name, grid, in_specs, out_specs, out_shape, scratch_shapes=()):
    return pl.pallas_call(
        kernel_fn,
        grid_spec=pltpu.PrefetchScalarGridSpec(num_scalar_prefetch=1, grid=grid, in_specs=in_specs,
                                               out_specs=out_specs, scratch_shapes=scratch_shapes),
        out_shape=out_shape,
        compiler_params=pltpu.CompilerParams(dimension_semantics=("parallel",) * len(grid),
                                             vmem_limit_bytes=VMEM_LIMIT),
        name=name)


def _dot(a, b):
    return jnp.dot(a, b, preferred_element_type=F32)


def _dot_nt(a, b):
    return lax.dot_general(a, b, (((1,), (1,)), ((), ())), preferred_element_type=F32)


def _inv_rms(y, n):
    return lax.rsqrt(jnp.sum(y * y, axis=-1, keepdims=True) * (1.0 / n) + EPS)


def _rope64(y, cos, sin):
    return y * cos + pltpu.roll(y, HALF_LANES, axis=1) * sin


ROPE_FREQS = DIFF_HEAD_DIM // 2
ROPE_GROUPS = LANES // ROPE_FREQS


def _rope_table_kernel(pos_ref, invf_ref, cos_ref, sin_ref):
    ang = pos_ref[...].astype(F32) * invf_ref[...]
    cos, sin = jnp.cos(ang), jnp.sin(ang)
    lane = lax.broadcasted_iota(jnp.int32, ang.shape, 1)
    sign = jnp.where(lane < HALF_LANES, -1.0, 1.0)

    def spread(x, k):
        if k:
            x = pltpu.roll(x, LANES - ROPE_FREQS * k, axis=1)
        x = jnp.where(lane < ROPE_FREQS, x, pltpu.roll(x, ROPE_FREQS, axis=1))
        return jnp.where(lane < HALF_LANES, x, pltpu.roll(x, HALF_LANES, axis=1))

    for k in range(ROPE_GROUPS):
        cos_ref[k] = spread(cos, k)
        sin_ref[k] = spread(sin, k) * sign


def _rope_tables(positions):
    t = positions.size
    rows = t // ROPE_GROUPS
    inv_freq = ROPE_THETA ** (-jnp.arange(0, DIFF_HEAD_DIM, 2, dtype=F32) / DIFF_HEAD_DIM)
    inv_freq = jnp.concatenate([inv_freq[0::2], inv_freq[1::2]])
    pos_rep = jnp.repeat(positions.reshape(ROPE_GROUPS, rows).T, ROPE_FREQS, axis=1)
    invf = jnp.tile(inv_freq, ROPE_GROUPS)[None, :]
    tr = min(1024, rows)
    out_blk = pl.BlockSpec((ROPE_GROUPS, tr, LANES), lambda i: (0, i, 0))
    cos, sin = pl.pallas_call(
        _rope_table_kernel,
        grid=(rows // tr,),
        in_specs=[pl.BlockSpec((tr, LANES), lambda i: (i, 0)),
                  pl.BlockSpec((1, LANES), lambda i: (0, 0))],
        out_specs=[out_blk] * 2,
        out_shape=[jax.ShapeDtypeStruct((ROPE_GROUPS, rows, LANES), F32)] * 2,
        name="rope_tables",
    )(pos_rep, invf)
    return cos.reshape(t, LANES), sin.reshape(t, LANES)


def _inproj_kernel(_layer_ref, x_ref, cosd_ref, sind_ref,
                   gmix_ref, win_ref, bgate_ref, dqg_ref, dkg_ref,
                   qag_ref, wqb_ref, kvag_ref, wkvk_ref, wkvv_ref, mqg_ref, mkg_ref, xqg_ref,
                   dq_ref, dk_ref, dv_ref, mq_ref, mk_ref, mv_ref, xq_ref, gate_ref):
    x = x_ref[...]
    h = (x * _inv_rms(x, D_MODEL) * gmix_ref[...]).astype(BF16)
    tm = x.shape[0]
    rs = slice(None)

    cosd, sind = cosd_ref[...], sind_ref[...]
    lane = lax.broadcasted_iota(jnp.int32, (tm, LANES), 1)
    is_map0 = (lane & 32) == 0
    is_mla_rope = (lane & (HALF_LANES - MLA_ROPE_DIM // 2)) == 0
    cosm = jnp.where(is_mla_rope, cosd, 1.0)
    sinm = jnp.where(is_mla_rope, sind, 0.0)

    def diff_qk(col0, gain_ref, out_ref):
        y = _dot(h, win_ref[:, col0:col0 + W_HEADS])
        gain = gain_ref[...]
        for hd in range(DIFF_HEADS):
            yh = y[:, hd * LANES:(hd + 1) * LANES]
            sq = yh * yh
            s_all = jnp.sum(sq, axis=-1, keepdims=True)
            s0 = jnp.sum(jnp.where(is_map0, sq, 0.0), axis=-1, keepdims=True)
            r = lax.rsqrt(jnp.where(is_map0, s0, s_all - s0) * (1.0 / DIFF_HEAD_DIM) + EPS)
            yn = yh * r * gain
            out_ref[rs, hd * LANES:(hd + 1) * LANES] = _rope64(yn, cosd, sind).astype(BF16)

    def diff_v():
        dv_ref[rs, :] = _dot(h, win_ref[:, C_DV:C_DV + W_HEADS]).astype(BF16)

    def gate(br):
        sl = slice(br * D_MODEL, (br + 1) * D_MODEL)
        z = _dot(h, win_ref[:, C_GATE + br * D_MODEL:C_GATE + (br + 1) * D_MODEL]) + bgate_ref[:, sl]
        gate_ref[rs, sl] = (0.5 * jnp.tanh(0.5 * z) + 0.5).astype(BF16)

    diff_qk(C_DQ, dqg_ref, dq_ref)
    diff_qk(C_DK, dkg_ref, dk_ref)
    diff_v()

    lat = _dot(h, win_ref[:, C_CQ:C_XQ])
    cq = lat[:, 0:MLA_Q_LORA]
    ckv = lat[:, MLA_Q_LORA:MLA_Q_LORA + MLA_KV_LORA]
    kr = lat[:, MLA_Q_LORA + MLA_KV_LORA:]
    cqn = (cq * _inv_rms(cq, MLA_Q_LORA) * qag_ref[...]).astype(BF16)
    ckvn = (ckv * _inv_rms(ckv, MLA_KV_LORA) * kvag_ref[...]).astype(BF16)
    kr = _rope64(kr, cosm, sinm)

    q = _dot(cqn, wqb_ref[...])
    kn = _dot(ckvn, wkvk_ref[...])
    mv_ref[rs, :] = _dot(ckvn, wkvv_ref[...]).astype(BF16)
    mqg, mkg = mqg_ref[...], mkg_ref[...]
    cosm_g, sinm_g = cosm * mqg, sinm * mqg
    for hd in range(MLA_HEADS):
        sl = slice(hd * LANES, (hd + 1) * LANES)
        qh = q[:, sl]
        mq_ref[rs, sl] = (_rope64(qh, cosm_g, sinm_g) * _inv_rms(qh, MLA_QK_DIM)).astype(BF16)
        kh = kn[:, sl] + kr
        mk_ref[rs, sl] = (kh * _inv_rms(kh, MLA_QK_DIM) * mkg).astype(BF16)

    xq = _dot(h, win_ref[:, C_XQ:C_GATE])
    xqg = xqg_ref[...]
    for hd in range(CROSS_HEADS):
        sl = slice(hd * LANES, (hd + 1) * LANES)
        yh = xq[:, sl]
        xq_ref[rs, sl] = (yh * _inv_rms(yh, CROSS_HEAD_DIM) * xqg).astype(BF16)
    for br in range(3):
        gate(br)


def _inproj(layer, x, tabs, p):
    t = x.shape[0]
    tm = min(TM_IN, t)
    row = lambda w: pl.BlockSpec((tm, w), lambda i, l: (i, 0))
    consts = [p["g_mix"], p["w_in"], p["b_gate"], p["dq_gain"], p["dk_gain"],
              p["qa_gain"], p["w_qb"], p["kva_gain"], p["w_kvk"], p["w_kvv"],
              p["mq_gain"], p["mk_gain"], p["xq_gain"]]
    out_widths = [W_HEADS] * 5 + [MLA_HEADS * MLA_V_DIM, CROSS_HEADS * CROSS_HEAD_DIM, 3 * D_MODEL]
    return _layer_call(
        _inproj_kernel, "inproj", (t // tm,),
        in_specs=[row(D_MODEL)] + [row(LANES)] * 2 + [_layer_spec(c.shape) for c in consts],
        out_specs=[row(w) for w in out_widths],
        out_shape=[jax.ShapeDtypeStruct((t, w), BF16) for w in out_widths],
    )(layer, x, *tabs, *consts)


def _causal_bias_t(tq):
    r = lax.broadcasted_iota(jnp.int32, (tq, tq), 0)
    c = lax.broadcasted_iota(jnp.int32, (tq, tq), 1)
    return jnp.where(r <= c, 0.0, -jnp.inf).astype(F32)


def _fold_rows(a, op):
    n, w = a.shape
    return op(op(a.reshape(n // SUBLANES, SUBLANES, w), axis=0), axis=0, keepdims=True)


def _causal_attention(jobs, tq, k_ref, s_ref, p_ref):
    bias_t = _causal_bias_t(tq)
    n_jobs = len(jobs)

    def scores(j):
        i, make_kq, _, _ = jobs[j]
        lo, hi = i * tq, (i + 1) * tq
        s_buf = s_ref.at[j % 2]
        maxima = []
        for k_sl, q_rows, s_sl in make_kq():
            reps = q_rows.shape[0] // tq
            bias = bias_t if reps == 1 else jnp.concatenate([bias_t] * reps, axis=1)
            s = _dot_nt(k_ref[lo:hi, k_sl], q_rows) + bias
            s_buf[lo:hi, s_sl] = s
            m = _fold_rows(s, jnp.max)
            if lo > 0:
                s = _dot_nt(k_ref[0:lo, k_sl], q_rows)
                s_buf[0:lo, s_sl] = s
                m = jnp.maximum(m, _fold_rows(s, jnp.max))
            maxima.append(m)
        return maxima[0] if len(maxima) == 1 else jnp.concatenate(maxima, axis=1)

    def softmax(j, m):
        hi = (jobs[j][0] + 1) * tq
        s_buf, p_buf = s_ref.at[j % 2], p_ref.at[j % 2]
        for r in range(0, hi, SM_ROWS):
            p_buf[r:r + SM_ROWS, :] = jnp.exp2(s_buf[r:r + SM_ROWS, :] - m).astype(BF16)

    def pv(j):
        i, _, vt_ref, finish = jobs[j]
        hi = (i + 1) * tq
        acc = _dot(vt_ref[:, 0:hi], p_ref[j % 2, 0:hi, :])
        finish(acc[0:LANES], acc[LANES:LANES + 1])

    m_next = scores(0)
    for j in range(n_jobs):
        m_cur = m_next
        if j + 1 < n_jobs:
            m_next = scores(j + 1)
        softmax(j, m_cur)
        if j > 0:
            pv(j - 1)
    pv(n_jobs - 1)


def _job_order(job, n_groups, n_tiles):
    return [job(g, i) for i in range(n_tiles) for g in range(n_groups)]


VT_ROWS = LANES + 16


def _store_vt(v_ref, vt_ref):
    for g in range(vt_ref.shape[0]):
        vt_ref[g, 0:LANES, :] = v_ref[:, g * LANES:(g + 1) * LANES].astype(F32).T.astype(BF16)
        vt_ref[g, LANES:VT_ROWS, :] = jnp.ones((VT_ROWS - LANES, vt_ref.shape[2]), BF16)


def _diff_attn_kernel(_layer_ref, lam_ref, og_ref, q_ref, k_ref, v_ref, o_ref, vt_ref, s_ref, p_ref):
    s_len = q_ref.shape[0]
    tq = TQ
    lane = lax.broadcasted_iota(jnp.int32, (tq, LANES), 1)
    is_map0 = (lane & 32) == 0
    _store_vt(v_ref, vt_ref)

    lv = lam_ref[...]
    lam = (jnp.exp(jnp.sum(lv[0:1] * lv[1:2], axis=-1, keepdims=True))
           - jnp.exp(jnp.sum(lv[2:3] * lv[3:4], axis=-1, keepdims=True))
           + lv[4:5, 0:1])
    og = og_ref[...] * (1.0 - lv[4:5, 0:1])
    og = jnp.concatenate([og] * (tq // LANES), axis=1)

    def job(g, i):
        cols = slice(g * LANES, (g + 1) * LANES)
        rows = slice(i * tq, (i + 1) * tq)

        def make_kq():
            q = q_ref[rows, cols]
            zero = jnp.zeros_like(q)
            qq = jnp.concatenate([jnp.where(is_map0, q, zero), jnp.where(is_map0, zero, q)], axis=0)
            return [(cols, qq, slice(None))]

        def finish(acc, l):
            o = acc / l
            o = o[:, 0:tq] - lam * o[:, tq:2 * tq]
            r = lax.rsqrt(jnp.sum(o * o, axis=0, keepdims=True) * (1.0 / (2 * DIFF_HEAD_DIM)) + EPS)
            o_ref[rows, cols] = (o * r * og).T.astype(BF16)

        return i, make_kq, vt_ref.at[g], finish

    jobs = _job_order(job, vt_ref.shape[0], s_len // tq)
    _causal_attention(jobs, tq, k_ref, s_ref, p_ref)


def _attn_scratch(seq):
    return [pltpu.VMEM((ATTN_GROUPS, VT_ROWS, seq), BF16),
            pltpu.VMEM((2, seq, 2 * TQ), F32),
            pltpu.VMEM((2, seq, 2 * TQ), BF16)]


def _diff_attn(layer, p, dq, dk, dv, batch, seq):
    blk = pl.BlockSpec((None, seq, ATTN_GROUPS * LANES), lambda b, h, l: (b, 0, h))
    shp = (batch, seq, W_HEADS)
    out = _layer_call(
        _diff_attn_kernel, "diff_attn", (batch, DIFF_HEADS // ATTN_GROUPS),
        in_specs=[_layer_spec(p["lam_tab"].shape), _layer_spec(p["out_gain_rows"].shape), blk, blk, blk],
        out_specs=blk,
        out_shape=jax.ShapeDtypeStruct(shp, BF16),
        scratch_shapes=_attn_scratch(seq),
    )(layer, p["lam_tab"], p["out_gain_rows"], dq.reshape(shp), dk.reshape(shp), dv.reshape(shp))
    return out.reshape(batch * seq, W_HEADS)


def _mla_attn_kernel(q_ref, k_ref, v_ref, o_ref, vt_ref, s_ref, p_ref):
    s_len = q_ref.shape[0]
    tq = TQ
    _store_vt(v_ref, vt_ref)
    row = lax.broadcasted_iota(jnp.int32, (LANES, tq), 0)
    is_head0 = row < MLA_V_DIM

    def job(g, i):
        rows = slice(i * tq, (i + 1) * tq)

        def make_kq():
            heads = [slice((2 * g + hd) * LANES, (2 * g + hd + 1) * LANES) for hd in range(2)]
            return [(heads[hd], q_ref[rows, heads[hd]], slice(hd * tq, (hd + 1) * tq)) for hd in range(2)]

        def finish(acc, l):
            o = acc / l
            o = jnp.where(is_head0, o[:, 0:tq], o[:, tq:2 * tq])
            o_ref[rows, g * LANES:(g + 1) * LANES] = o.T.astype(BF16)

        return i, make_kq, vt_ref.at[g], finish

    jobs = _job_order(job, vt_ref.shape[0], s_len // tq)
    _causal_attention(jobs, tq, k_ref, s_ref, p_ref)


def _mla_attn(mq, mk, mv, batch, seq):
    pairs = ATTN_GROUPS
    qk_blk = pl.BlockSpec((None, seq, pairs * 2 * LANES), lambda b, h: (b, 0, h))
    v_blk = pl.BlockSpec((None, seq, pairs * 2 * MLA_V_DIM), lambda b, h: (b, 0, h))
    qk_shp = (batch, seq, MLA_HEADS * LANES)
    v_shp = (batch, seq, MLA_HEADS * MLA_V_DIM)
    out = pl.pallas_call(
        _mla_attn_kernel,
        grid=(batch, MLA_HEADS // (2 * pairs)),
        in_specs=[qk_blk, qk_blk, v_blk],
        out_specs=v_blk,
        out_shape=jax.ShapeDtypeStruct(v_shp, BF16),
        scratch_shapes=_attn_scratch(seq),
        compiler_params=pltpu.CompilerParams(dimension_semantics=("parallel", "parallel"),
                                             vmem_limit_bytes=VMEM_LIMIT),
        name="mla_attn",
    )(mq.reshape(qk_shp), mk.reshape(qk_shp), mv.reshape(v_shp))
    return out.reshape(batch * seq, MLA_HEADS * MLA_V_DIM)


def _mem_kv_kernel(_layer_ref, mem_ref, mg_ref, w_ref, kg_ref, k_ref, v_ref):
    m = mem_ref[...]
    mn = (m * _inv_rms(m, D_MODEL) * mg_ref[...]).astype(BF16)
    kv = _dot(mn, w_ref[...])
    width = CROSS_HEADS * CROSS_HEAD_DIM
    kg = kg_ref[...]
    for hd in range(CROSS_HEADS):
        sl = slice(hd * LANES, (hd + 1) * LANES)
        kh = kv[:, sl]
        k_ref[:, sl] = (kh * _inv_rms(kh, CROSS_HEAD_DIM) * kg).astype(BF16)
    v_ref[...] = kv[:, width:].astype(BF16)


def _mem_kv(layer, mem2d, p):
    rows = mem2d.shape[0]
    width = CROSS_HEADS * CROSS_HEAD_DIM
    tm = min(512, rows)
    return _layer_call(
        _mem_kv_kernel, "mem_kv", (rows // tm,),
        in_specs=[pl.BlockSpec((tm, D_MODEL), lambda i, l: (i, 0)), _layer_spec(p["mem_gain"].shape),
                  _layer_spec(p["w_mem_kv"].shape), _layer_spec(p["ck_gain"].shape)],
        out_specs=[pl.BlockSpec((tm, width), lambda i, l: (i, 0))] * 2,
        out_shape=[jax.ShapeDtypeStruct((rows, width), BF16)] * 2,
    )(layer, mem2d, p["mem_gain"], p["w_mem_kv"], p["ck_gain"])


FF_CHUNK = 1024


def _post_kernel(_layer_ref, x_ref, od_ref, om_ref, xq_ref, ck_ref, cv_ref, g_ref,
                 wdo_ref, wmo_ref, wco_ref, wout_ref, gmlp_ref, w1_ref, w2_ref, o_ref):
    heads = []
    for hd in range(CROSS_HEADS):
        sl = slice(hd * LANES, (hd + 1) * LANES)
        s = _dot_nt(xq_ref[:, sl], ck_ref[:, sl])
        p = jnp.exp2(s - jnp.max(s, axis=-1, keepdims=True))
        l = jnp.sum(p, axis=-1, keepdims=True)
        heads.append((_dot(p.astype(BF16), cv_ref[:, sl]) / l).astype(BF16))
    oc = jnp.concatenate(heads, axis=1)

    merged = (g_ref[:, 0:D_MODEL].astype(F32) * _dot(od_ref[...], wdo_ref[...])
              + g_ref[:, D_MODEL:2 * D_MODEL].astype(F32) * _dot(om_ref[...], wmo_ref[...])
              + g_ref[:, 2 * D_MODEL:].astype(F32) * _dot(oc, wco_ref[...]))
    x = x_ref[...] + _dot(merged.astype(BF16), wout_ref[...])

    h = (x * _inv_rms(x, D_MODEL) * gmlp_ref[...]).astype(BF16)
    y = x
    for c in range(D_FF // FF_CHUNK):
        sl = slice(c * FF_CHUNK, (c + 1) * FF_CHUNK)
        u = jnp.maximum(_dot(h, w1_ref[:, sl]), 0.0)
        y = y + _dot((u * u).astype(BF16), w2_ref[sl, :])
    o_ref[...] = y


def _post(layer, x, od, om, xq, ck, cv, gates, p, seq):
    t = x.shape[0]
    tm = min(TM_OUT, seq)
    tiles_per_batch = seq // tm
    width = CROSS_HEADS * CROSS_HEAD_DIM
    row = lambda w: pl.BlockSpec((tm, w), lambda i, l: (i, 0))
    mem_blk = pl.BlockSpec((MEM_LEN, width), lambda i, l: (i // tiles_per_batch, 0))
    consts = [p["w_diff_o"], p["w_mla_o"], p["w_cross_o"], p["w_out"], p["g_mlp"], p["w_mlp1"], p["w_mlp2"]]
    return _layer_call(
        _post_kernel, "post", (t // tm,),
        in_specs=[row(D_MODEL), row(od.shape[1]), row(om.shape[1]), row(width), mem_blk, mem_blk,
                  row(gates.shape[1])] + [_layer_spec(c.shape) for c in consts],
        out_specs=row(D_MODEL),
        out_shape=jax.ShapeDtypeStruct((t, D_MODEL), F32),
    )(layer, x, od, om, xq, ck, cv, gates, *consts)


def _pack_params(w_in, b_gate, g_mix, diff_q_gain, diff_k_gain, diff_lambda, diff_out_gain, w_diff_o,
                 mla_qa_gain, w_mla_qb, mla_kva_gain, w_mla_kvb, mla_q_gain, mla_k_gain, w_mla_o,
                 mem_gain, w_mem_kv, cross_q_gain, cross_k_gain, w_cross_o, w_out, g_mlp, w_mlp1, w_mlp2):
    nl = w_in.shape[0]
    d = D_MODEL
    half = DIFF_HEAD_DIM // 2
    o_dq, o_dk, o_dv = 0, 1024, 2048
    o_cq = 3072
    o_ckv = o_cq + MLA_Q_LORA
    o_kr = o_ckv + MLA_KV_LORA
    o_xq = o_kr + MLA_ROPE_DIM

    def diff_cols(w):
        r = w.shape[1]
        return w.reshape(nl, r, DIFF_HEADS, 2, 2, half // 2, 2).transpose(0, 1, 2, 4, 3, 6, 5).reshape(
            nl, r, DIFF_HEADS * LANES)

    def diff_gain(g, scale):
        g = g.reshape(nl, 2, 1, half // 2, 2).transpose(0, 1, 2, 4, 3)
        g = jnp.broadcast_to(g, (nl, 2, 2, 2, half // 2)).reshape(nl, 1, LANES)
        return (g * scale).astype(F32)

    rh = MLA_ROPE_DIM // 2

    def mla_head_layout(nope, rope):
        pad = jnp.zeros(nope.shape[:-1] + (LANES - MLA_QK_DIM,), nope.dtype)
        return jnp.concatenate([rope[..., :rh], nope[..., :HALF_LANES - rh], rope[..., rh:],
                                nope[..., HALF_LANES - rh:], pad], axis=-1)

    kr_w = w_in[:, :, o_kr:o_xq]
    zpad = jnp.zeros((nl, d, HALF_LANES - rh), F32)
    kr_cols = jnp.concatenate([kr_w[..., :rh], zpad, kr_w[..., rh:], zpad], axis=-1)
    w_in_p = jnp.concatenate([diff_cols(w_in[:, :, o_dq:o_dk]).astype(BF16),
                              diff_cols(w_in[:, :, o_dk:o_dv]).astype(BF16),
                              w_in[:, :, o_dv:o_kr].astype(BF16), kr_cols.astype(BF16),
                              w_in[:, :, o_xq:].astype(BF16)], axis=-1)

    wqb = w_mla_qb.reshape(nl, MLA_Q_LORA, MLA_HEADS, MLA_QK_DIM)
    w_qb_p = mla_head_layout(wqb[..., :MLA_NOPE_DIM], wqb[..., MLA_NOPE_DIM:]).reshape(
        nl, MLA_Q_LORA, MLA_HEADS * LANES).astype(BF16)
    wkvb = w_mla_kvb.reshape(nl, MLA_KV_LORA, MLA_HEADS, MLA_NOPE_DIM + MLA_V_DIM)
    k_nope = wkvb[..., :MLA_NOPE_DIM]
    w_kvk_p = mla_head_layout(k_nope, jnp.zeros(k_nope.shape[:-1] + (MLA_ROPE_DIM,), F32)).reshape(
        nl, MLA_KV_LORA, MLA_HEADS * LANES).astype(BF16)
    w_kvv_p = wkvb[..., MLA_NOPE_DIM:].reshape(nl, MLA_KV_LORA, MLA_HEADS * MLA_V_DIM).astype(BF16)

    def mla_gain(g, scale):
        return (mla_head_layout(g[:, :MLA_NOPE_DIM], g[:, MLA_NOPE_DIM:]) * scale)[:, None, :].astype(F32)

    lam_init = jnp.asarray([0.8 - 0.6 * math.exp(-0.3 * l) for l in range(nl)], F32)
    lam_tab = jnp.zeros((nl, SUBLANES, LANES), F32)
    lam_tab = lam_tab.at[:, 0:4, 0:DIFF_HEAD_DIM].set(diff_lambda.astype(F32))
    lam_tab = lam_tab.at[:, 4, :].set(lam_init[:, None])
    out_gain_rows = jnp.broadcast_to(diff_out_gain[:, :, None], (nl, 2 * DIFF_HEAD_DIM, LANES)).astype(F32)

    row = lambda g: g[:, None, :]
    return dict(
        g_mix=row(g_mix), w_in=w_in_p, b_gate=row(b_gate),
        dq_gain=diff_gain(diff_q_gain, DIFF_HEAD_DIM ** -0.5 * LOG2E), dk_gain=diff_gain(diff_k_gain, 1.0),
        qa_gain=row(mla_qa_gain), w_qb=w_qb_p, kva_gain=row(mla_kva_gain),
        w_kvk=w_kvk_p, w_kvv=w_kvv_p,
        mq_gain=mla_gain(mla_q_gain, MLA_QK_DIM ** -0.5 * LOG2E), mk_gain=mla_gain(mla_k_gain, 1.0),
        xq_gain=row(cross_q_gain * (CROSS_HEAD_DIM ** -0.5 * LOG2E)),
        lam_tab=lam_tab, out_gain_rows=out_gain_rows,
        w_diff_o=w_diff_o.astype(BF16), w_mla_o=w_mla_o.astype(BF16),
        mem_gain=row(mem_gain), w_mem_kv=w_mem_kv.astype(BF16), ck_gain=row(cross_k_gain),
        w_cross_o=w_cross_o.astype(BF16), w_out=w_out.astype(BF16),
        g_mlp=row(g_mlp), w_mlp1=w_mlp1.astype(BF16), w_mlp2=w_mlp2.astype(BF16),
    )


def kernel(x, mem, positions, g_mix, w_in, b_gate, diff_q_gain, diff_k_gain, diff_lambda, diff_out_gain,
           w_diff_o, mla_qa_gain, w_mla_qb, mla_kva_gain, w_mla_kvb, mla_q_gain, mla_k_gain, w_mla_o,
           mem_gain, w_mem_kv, cross_q_gain, cross_k_gain, w_cross_o, w_out, g_mlp, w_mlp1, w_mlp2):
    batch, seq, d = x.shape
    tabs = _rope_tables(positions)
    packed = _pack_params(w_in, b_gate, g_mix, diff_q_gain, diff_k_gain, diff_lambda, diff_out_gain,
                          w_diff_o, mla_qa_gain, w_mla_qb, mla_kva_gain, w_mla_kvb, mla_q_gain, mla_k_gain,
                          w_mla_o, mem_gain, w_mem_kv, cross_q_gain, cross_k_gain, w_cross_o, w_out,
                          g_mlp, w_mlp1, w_mlp2)
    xt = x.reshape(batch * seq, d)
    mem2d = mem.reshape(batch * mem.shape[1], d)
    for l in range(w_in.shape[0]):
        layer = jnp.full((1,), l, jnp.int32)
        dq, dk, dv, mq, mk, mv, xq, gates = _inproj(layer, xt, tabs, packed)
        od = _diff_attn(layer, packed, dq, dk, dv, batch, seq)
        om = _mla_attn(mq, mk, mv, batch, seq)
        ck, cv = _mem_kv(layer, mem2d, packed)
        xt = _post(layer, xt, od, om, xq, ck, cv, gates, packed, seq)
    return xt.reshape(batch, seq, d)
```

```python
import math

import jax
import jax.numpy as jnp
from jax import lax
from jax.experimental import pallas as pl
from jax.experimental.pallas import tpu as pltpu

D_MODEL = 1024
DEPTH = 4
MEM_LEN = 256
ROPE_THETA = 10000.0
EPS = 1e-6

DIFF_HEADS = 8
DIFF_HEAD_DIM = 64
MLA_HEADS = 8
MLA_Q_LORA = 384
MLA_KV_LORA = 256
MLA_NOPE_DIM = 64
MLA_ROPE_DIM = 32
MLA_V_DIM = 64
MLA_QK_DIM = MLA_NOPE_DIM + MLA_ROPE_DIM
CROSS_HEADS = 4
CROSS_HEAD_DIM = 128
D_FF = 4 * D_MODEL

LANES = 128
SUBLANES = 8
HALF_LANES = LANES // 2
LOG2E = math.log2(math.e)

W_HEADS = DIFF_HEADS * LANES
C_DQ = 0
C_DK = C_DQ + W_HEADS
C_DV = C_DK + W_HEADS
C_CQ = C_DV + W_HEADS
C_CKV = C_CQ + MLA_Q_LORA
C_KR = C_CKV + MLA_KV_LORA
C_XQ = C_KR + LANES
C_GATE = C_XQ + CROSS_HEADS * CROSS_HEAD_DIM
IN_WIDTH_PACKED = C_GATE + 3 * D_MODEL

TM_IN = 512
TM_OUT = 512
TQ = 256
ATTN_GROUPS = 2
SM_ROWS = 32
VMEM_LIMIT = 56 * 1024 * 1024

BF16 = jnp.bfloat16
F32 = jnp.float32


def _layer_spec(shape):
    nd = len(shape) - 1
    return pl.BlockSpec((None,) + tuple(shape[1:]), lambda *args: (args[-1][0],) + (0,) * nd,
                        pipeline_mode=pl.Buffered(1))


def _layer_call(kernel_fn, name, grid, in_specs, out_specs, out_shape, scratch_shapes=()):
    return pl.pallas_call(
        kernel_fn,
        grid_spec=pltpu.PrefetchScalarGridSpec(num_scalar_prefetch=1, grid=grid, in_specs=in_specs,
                                               out_specs=out_specs, scratch_shapes=scratch_shapes),
        out_shape=out_shape,
        compiler_params=pltpu.CompilerParams(dimension_semantics=("parallel",) * len(grid),
                                             vmem_limit_bytes=VMEM_LIMIT),
        name=name)


def _dot(a, b):
    return jnp.dot(a, b, preferred_element_type=F32)


def _dot_nt(a, b):
    return lax.dot_general(a, b, (((1,), (1,)), ((), ())), preferred_element_type=F32)


def _inv_rms(y, n):
    return lax.rsqrt(jnp.sum(y * y, axis=-1, keepdims=True) * (1.0 / n) + EPS)


def _rope64(y, cos, sin):
    return y * cos + pltpu.roll(y, HALF_LANES, axis=1) * sin


ROPE_FREQS = DIFF_HEAD_DIM // 2
ROPE_GROUPS = LANES // ROPE_FREQS


def _rope_table_kernel(pos_ref, invf_ref, cos_ref, sin_ref):
    ang = pos_ref[...].astype(F32) * invf_ref[...]
    cos, sin = jnp.cos(ang), jnp.sin(ang)
    lane = lax.broadcasted_iota(jnp.int32, ang.shape, 1)
    sign = jnp.where(lane < HALF_LANES, -1.0, 1.0)

    def spread(x, k):
        if k:
            x = pltpu.roll(x, LANES - ROPE_FREQS * k, axis=1)
        x = jnp.where(lane < ROPE_FREQS, x, pltpu.roll(x, ROPE_FREQS, axis=1))
        return jnp.where(lane < HALF_LANES, x, pltpu.roll(x, HALF_LANES, axis=1))

    for k in range(ROPE_GROUPS):
        cos_ref[k] = spread(cos, k)
        sin_ref[k] = spread(sin, k) * sign


def _rope_tables(positions):
    t = positions.size
    rows = t // ROPE_GROUPS
    inv_freq = ROPE_THETA ** (-jnp.arange(0, DIFF_HEAD_DIM, 2, dtype=F32) / DIFF_HEAD_DIM)
    inv_freq = jnp.concatenate([inv_freq[0::2], inv_freq[1::2]])
    pos_rep = jnp.repeat(positions.reshape(ROPE_GROUPS, rows).T, ROPE_FREQS, axis=1)
    invf = jnp.tile(inv_freq, ROPE_GROUPS)[None, :]
    tr = min(1024, rows)
    out_blk = pl.BlockSpec((ROPE_GROUPS, tr, LANES), lambda i: (0, i, 0))
    cos, sin = pl.pallas_call(
        _rope_table_kernel,
        grid=(rows // tr,),
        in_specs=[pl.BlockSpec((tr, LANES), lambda i: (i, 0)),
                  pl.BlockSpec((1, LANES), lambda i: (0, 0))],
        out_specs=[out_blk] * 2,
        out_shape=[jax.ShapeDtypeStruct((ROPE_GROUPS, rows, LANES), F32)] * 2,
        name="rope_tables",
    )(pos_rep, invf)
    return cos.reshape(t, LANES), sin.reshape(t, LANES)


def _inproj_kernel(_layer_ref, x_ref, cosd_ref, sind_ref,
                   gmix_ref, win_ref, bgate_ref, dqg_ref, dkg_ref,
                   qag_ref, wqb_ref, kvag_ref, wkvk_ref, wkvv_ref, mqg_ref, mkg_ref, xqg_ref,
                   dq_ref, dk_ref, dv_ref, mq_ref, mk_ref, mv_ref, xq_ref, gate_ref):
    x = x_ref[...]
    h = (x * _inv_rms(x, D_MODEL) * gmix_ref[...]).astype(BF16)
    tm = x.shape[0]
    rs = slice(None)

    cosd, sind = cosd_ref[...], sind_ref[...]
    lane = lax.broadcasted_iota(jnp.int32, (tm, LANES), 1)
    is_map0 = (lane & 32) == 0
    is_mla_rope = (lane & (HALF_LANES - MLA_ROPE_DIM // 2)) == 0
    cosm = jnp.where(is_mla_rope, cosd, 1.0)
    sinm = jnp.where(is_mla_rope, sind, 0.0)

    def diff_qk(col0, gain_ref, out_ref):
        y = _dot(h, win_ref[:, col0:col0 + W_HEADS])
        gain = gain_ref[...]
        for hd in range(DIFF_HEADS):
            yh = y[:, hd * LANES:(hd + 1) * LANES]
            sq = yh * yh
            s_all = jnp.sum(sq, axis=-1, keepdims=True)
            s0 = jnp.sum(jnp.where(is_map0, sq, 0.0), axis=-1, keepdims=True)
            r = lax.rsqrt(jnp.where(is_map0, s0, s_all - s0) * (1.0 / DIFF_HEAD_DIM) + EPS)
            yn = yh * r * gain
            out_ref[rs, hd * LANES:(hd + 1) * LANES] = _rope64(yn, cosd, sind).astype(BF16)

    def diff_v():
        dv_ref[rs, :] = _dot(h, win_ref[:, C_DV:C_DV + W_HEADS]).astype(BF16)

    def gate(br):
        sl = slice(br * D_MODEL, (br + 1) * D_MODEL)
        z = _dot(h, win_ref[:, C_GATE + br * D_MODEL:C_GATE + (br + 1) * D_MODEL]) + bgate_ref[:, sl]
        gate_ref[rs, sl] = (0.5 * jnp.tanh(z) + 0.5).astype(BF16)

    diff_qk(C_DQ, dqg_ref, dq_ref)
    diff_qk(C_DK, dkg_ref, dk_ref)
    diff_v()

    lat = _dot(h, win_ref[:, C_CQ:C_XQ])
    cq = lat[:, 0:MLA_Q_LORA]
    ckv = lat[:, MLA_Q_LORA:MLA_Q_LORA + MLA_KV_LORA]
    kr = lat[:, MLA_Q_LORA + MLA_KV_LORA:]
    cqn = (cq * _inv_rms(cq, MLA_Q_LORA) * qag_ref[...]).astype(BF16)
    ckvn = (ckv * _inv_rms(ckv, MLA_KV_LORA) * kvag_ref[...]).astype(BF16)
    kr = _rope64(kr, cosm, sinm)

    q = _dot(cqn, wqb_ref[...])
    kn = _dot(ckvn, wkvk_ref[...])
    mv_ref[rs, :] = _dot(ckvn, wkvv_ref[...]).astype(BF16)
    mqg, mkg = mqg_ref[...], mkg_ref[...]
    cosm_g, sinm_g = cosm * mqg, sinm * mqg
    for hd in range(MLA_HEADS):
        sl = slice(hd * LANES, (hd + 1) * LANES)
        qh = q[:, sl]
        mq_ref[rs, sl] = (_rope64(qh, cosm_g, sinm_g) * _inv_rms(qh, MLA_QK_DIM)).astype(BF16)
        kh = kn[:, sl] + kr
        mk_ref[rs, sl] = (kh * _inv_rms(kh, MLA_QK_DIM) * mkg).astype(BF16)

    xq = _dot(h, win_ref[:, C_XQ:C_GATE])
    xqg = xqg_ref[...]
    for hd in range(CROSS_HEADS):
        sl = slice(hd * LANES, (hd + 1) * LANES)
        yh = xq[:, sl]
        xq_ref[rs, sl] = (yh * _inv_rms(yh, CROSS_HEAD_DIM) * xqg).astype(BF16)
    for br in range(3):
        gate(br)


def _inproj(layer, x, tabs, p):
    t = x.shape[0]
    tm = min(TM_IN, t)
    row = lambda w: pl.BlockSpec((tm, w), lambda i, l: (i, 0))
    consts = [p["g_mix"], p["w_in"], p["b_gate"], p["dq_gain"], p["dk_gain"],
              p["qa_gain"], p["w_qb"], p["kva_gain"], p["w_kvk"], p["w_kvv"],
              p["mq_gain"], p["mk_gain"], p["xq_gain"]]
    out_widths = [W_HEADS] * 5 + [MLA_HEADS * MLA_V_DIM, CROSS_HEADS * CROSS_HEAD_DIM, 3 * D_MODEL]
    return _layer_call(
        _inproj_kernel, "inproj", (t // tm,),
        in_specs=[row(D_MODEL)] + [row(LANES)] * 2 + [_layer_spec(c.shape) for c in consts],
        out_specs=[row(w) for w in out_widths],
        out_shape=[jax.ShapeDtypeStruct((t, w), BF16) for w in out_widths],
    )(layer, x, *tabs, *consts)


def _causal_bias_t(tq):
    r = lax.broadcasted_iota(jnp.int32, (tq, tq), 0)
    c = lax.broadcasted_iota(jnp.int32, (tq, tq), 1)
    return jnp.where(r <= c, 0.0, -jnp.inf).astype(F32)


def _fold_rows(a, op):
    n, w = a.shape
    return op(op(a.reshape(n // SUBLANES, SUBLANES, w), axis=0), axis=0, keepdims=True)


def _causal_attention(jobs, tq, k_ref, s_ref, p_ref):
    bias_t = _causal_bias_t(tq)
    n_jobs = len(jobs)

    def scores(j):
        i, make_kq, _, _ = jobs[j]
        lo, hi = i * tq, (i + 1) * tq
        s_buf = s_ref.at[j % 2]
        maxima = []
        for k_sl, q_rows, s_sl in make_kq():
            reps = q_rows.shape[0] // tq
            bias = bias_t if reps == 1 else jnp.concatenate([bias_t] * reps, axis=1)
            s = _dot_nt(k_ref[lo:hi, k_sl], q_rows) + bias
            s_buf[lo:hi, s_sl] = s
            m = _fold_rows(s, jnp.max)
            if lo > 0:
                s = _dot_nt(k_ref[0:lo, k_sl], q_rows)
                s_buf[0:lo, s_sl] = s
                m = jnp.maximum(m, _fold_rows(s, jnp.max))
            maxima.append(m)
        return maxima[0] if len(maxima) == 1 else jnp.concatenate(maxima, axis=1)

    def softmax(j, m):
        hi = (jobs[j][0] + 1) * tq
        s_buf, p_buf = s_ref.at[j % 2], p_ref.at[j % 2]
        for r in range(0, hi, SM_ROWS):
            p_buf[r:r + SM_ROWS, :] = jnp.exp2(s_buf[r:r + SM_ROWS, :] - m).astype(BF16)

    def pv(j):
        i, _, vt_ref, finish = jobs[j]
        hi = (i + 1) * tq
        acc = _dot(vt_ref[:, 0:hi], p_ref[j % 2, 0:hi, :])
        finish(acc[0:LANES], acc[LANES:LANES + 1])

    m_next = scores(0)
    for j in range(n_jobs):
        m_cur = m_next
        if j + 1 < n_jobs:
            m_next = scores(j + 1)
        softmax(j, m_cur)
        if j > 0:
            pv(j - 1)
    pv(n_jobs - 1)


def _job_order(job, n_groups, n_tiles):
    return [job(g, i) for i in range(n_tiles) for g in range(n_groups)]


VT_ROWS = LANES + 16


def _store_vt(v_ref, vt_ref):
    for g in range(vt_ref.shape[0]):
        vt_ref[g, 0:LANES, :] = v_ref[:, g * LANES:(g + 1) * LANES].astype(F32).T.astype(BF16)
        vt_ref[g, LANES:VT_ROWS, :] = jnp.ones((VT_ROWS - LANES, vt_ref.shape[2]), BF16)


def _diff_attn_kernel(_layer_ref, lam_ref, og_ref, q_ref, k_ref, v_ref, o_ref, vt_ref, s_ref, p_ref):
    s_len = q_ref.shape[0]
    tq = TQ
    lane = lax.broadcasted_iota(jnp.int32, (tq, LANES), 1)
    is_map0 = (lane & 32) == 0
    _store_vt(v_ref, vt_ref)

    lv = lam_ref[...]
    lam = (jnp.exp(jnp.sum(lv[0:1] * lv[1:2], axis=-1, keepdims=True))
           - jnp.exp(jnp.sum(lv[2:3] * lv[3:4], axis=-1, keepdims=True))
           + lv[4:5, 0:1])
    og = og_ref[...] * (1.0 - lv[4:5, 0:1])
    og = jnp.concatenate([og] * (tq // LANES), axis=1)

    def job(g, i):
        cols = slice(g * LANES, (g + 1) * LANES)
        rows = slice(i * tq, (i + 1) * tq)

        def make_kq():
            q = q_ref[rows, cols]
            zero = jnp.zeros_like(q)
            qq = jnp.concatenate([jnp.where(is_map0, q, zero), jnp.where(is_map0, zero, q)], axis=0)
            return [(cols, qq, slice(None))]

        def finish(acc, l):
            o = acc / l
            o = o[:, 0:tq] - lam * o[:, tq:2 * tq]
            r = lax.rsqrt(jnp.sum(o * o, axis=0, keepdims=True) * (1.0 / (2 * DIFF_HEAD_DIM)) + EPS)
            o_ref[rows, cols] = (o * r * og).T.astype(BF16)

        return i, make_kq, vt_ref.at[g], finish

    jobs = _job_order(job, vt_ref.shape[0], s_len // tq)
    _causal_attention(jobs, tq, k_ref, s_ref, p_ref)


def _attn_scratch(seq):
    return [pltpu.VMEM((ATTN_GROUPS, VT_ROWS, seq), BF16),
            pltpu.VMEM((2, seq, 2 * TQ), F32),
            pltpu.VMEM((2, seq, 2 * TQ), BF16)]


def _diff_attn(layer, p, dq, dk, dv, batch, seq):
    blk = pl.BlockSpec((None, seq, ATTN_GROUPS * LANES), lambda b, h, l: (b, 0, h))
    shp = (batch, seq, W_HEADS)
    out = _layer_call(
        _diff_attn_kernel, "diff_attn", (batch, DIFF_HEADS // ATTN_GROUPS),
        in_specs=[_layer_spec(p["lam_tab"].shape), _layer_spec(p["out_gain_rows"].shape), blk, blk, blk],
        out_specs=blk,
        out_shape=jax.ShapeDtypeStruct(shp, BF16),
        scratch_shapes=_attn_scratch(seq),
    )(layer, p["lam_tab"], p["out_gain_rows"], dq.reshape(shp), dk.reshape(shp), dv.reshape(shp))
    return out.reshape(batch * seq, W_HEADS)


def _mla_attn_kernel(q_ref, k_ref, v_ref, o_ref, vt_ref, s_ref, p_ref):
    s_len = q_ref.shape[0]
    tq = TQ
    _store_vt(v_ref, vt_ref)
    row = lax.broadcasted_iota(jnp.int32, (LANES, tq), 0)
    is_head0 = row < MLA_V_DIM

    def job(g, i):
        rows = slice(i * tq, (i + 1) * tq)

        def make_kq():
            heads = [slice((2 * g + hd) * LANES, (2 * g + hd + 1) * LANES) for hd in range(2)]
            return [(heads[hd], q_ref[rows, heads[hd]], slice(hd * tq, (hd + 1) * tq)) for hd in range(2)]

        def finish(acc, l):
            o = acc / l
            o = jnp.where(is_head0, o[:, 0:tq], o[:, tq:2 * tq])
            o_ref[rows, g * LANES:(g + 1) * LANES] = o.T.astype(BF16)

        return i, make_kq, vt_ref.at[g], finish

    jobs = _job_order(job, vt_ref.shape[0], s_len // tq)
    _causal_attention(jobs, tq, k_ref, s_ref, p_ref)


def _mla_attn(mq, mk, mv, batch, seq):
    pairs = ATTN_GROUPS
    qk_blk = pl.BlockSpec((None, seq, pairs * 2 * LANES), lambda b, h: (b, 0, h))
    v_blk = pl.BlockSpec((None, seq, pairs * 2 * MLA_V_DIM), lambda b, h: (b, 0, h))
    qk_shp = (batch, seq, MLA_HEADS * LANES)
    v_shp = (batch, seq, MLA_HEADS * MLA_V_DIM)
    out = pl.pallas_call(
        _mla_attn_kernel,
        grid=(batch, MLA_HEADS // (2 * pairs)),
        in_specs=[qk_blk, qk_blk, v_blk],
        out_specs=v_blk,
        out_shape=jax.ShapeDtypeStruct(v_shp, BF16),
        scratch_shapes=_attn_scratch(seq),
        compiler_params=pltpu.CompilerParams(dimension_semantics=("parallel", "parallel"),
                                             vmem_limit_bytes=VMEM_LIMIT),
        name="mla_attn",
    )(mq.reshape(qk_shp), mk.reshape(qk_shp), mv.reshape(v_shp))
    return out.reshape(batch * seq, MLA_HEADS * MLA_V_DIM)


def _mem_kv_kernel(_layer_ref, mem_ref, mg_ref, w_ref, kg_ref, k_ref, v_ref):
    m = mem_ref[...]
    mn = (m * _inv_rms(m, D_MODEL) * mg_ref[...]).astype(BF16)
    kv = _dot(mn, w_ref[...])
    width = CROSS_HEADS * CROSS_HEAD_DIM
    kg = kg_ref[...]
    for hd in range(CROSS_HEADS):
        sl = slice(hd * LANES, (hd + 1) * LANES)
        kh = kv[:, sl]
        k_ref[:, sl] = (kh * _inv_rms(kh, CROSS_HEAD_DIM) * kg).astype(BF16)
    v_ref[...] = kv[:, width:].astype(BF16)


def _mem_kv(layer, mem2d, p):
    rows = mem2d.shape[0]
    width = CROSS_HEADS * CROSS_HEAD_DIM
    tm = min(512, rows)
    return _layer_call(
        _mem_kv_kernel, "mem_kv", (rows // tm,),
        in_specs=[pl.BlockSpec((tm, D_MODEL), lambda i, l: (i, 0)), _layer_spec(p["mem_gain"].shape),
                  _layer_spec(p["w_mem_kv"].shape), _layer_spec(p["ck_gain"].shape)],
        out_specs=[pl.BlockSpec((tm, width), lambda i, l: (i, 0))] * 2,
        out_shape=[jax.ShapeDtypeStruct((rows, width), BF16)] * 2,
    )(layer, mem2d, p["mem_gain"], p["w_mem_kv"], p["ck_gain"])


FF_CHUNK = 1024


def _post_kernel(_layer_ref, x_ref, od_ref, om_ref, xq_ref, ck_ref, cv_ref, g_ref,
                 wdo_ref, wmo_ref, wco_ref, wout_ref, gmlp_ref, w1_ref, w2_ref, o_ref):
    heads = []
    for hd in range(CROSS_HEADS):
        sl = slice(hd * LANES, (hd + 1) * LANES)
        s = _dot_nt(xq_ref[:, sl], ck_ref[:, sl])
        p = jnp.exp2(s - jnp.max(s, axis=-1, keepdims=True))
        l = jnp.sum(p, axis=-1, keepdims=True)
        heads.append((_dot(p.astype(BF16), cv_ref[:, sl]) / l).astype(BF16))
    oc = jnp.concatenate(heads, axis=1)

    merged = (g_ref[:, 0:D_MODEL].astype(F32) * _dot(od_ref[...], wdo_ref[...])
              + g_ref[:, D_MODEL:2 * D_MODEL].astype(F32) * _dot(om_ref[...], wmo_ref[...])
              + g_ref[:, 2 * D_MODEL:].astype(F32) * _dot(oc, wco_ref[...]))
    x = x_ref[...] + _dot(merged.astype(BF16), wout_ref[...])

    h = (x * _inv_rms(x, D_MODEL) * gmlp_ref[...]).astype(BF16)
    y = x
    for c in range(D_FF // FF_CHUNK):
        sl = slice(c * FF_CHUNK, (c + 1) * FF_CHUNK)
        u = jnp.maximum(_dot(h, w1_ref[:, sl]), 0.0)
        y = y + _dot((u * u).astype(BF16), w2_ref[sl, :])
    o_ref[...] = y


def _post(layer, x, od, om, xq, ck, cv, gates, p, seq):
    t = x.shape[0]
    tm = min(TM_OUT, seq)
    tiles_per_batch = seq // tm
    width = CROSS_HEADS * CROSS_HEAD_DIM
    row = lambda w: pl.BlockSpec((tm, w), lambda i, l: (i, 0))
    mem_blk = pl.BlockSpec((MEM_LEN, width), lambda i, l: (i // tiles_per_batch, 0))
    consts = [p["w_diff_o"], p["w_mla_o"], p["w_cross_o"], p["w_out"], p["g_mlp"], p["w_mlp1"], p["w_mlp2"]]
    return _layer_call(
        _post_kernel, "post", (t // tm,),
        in_specs=[row(D_MODEL), row(od.shape[1]), row(om.shape[1]), row(width), mem_blk, mem_blk,
                  row(gates.shape[1])] + [_layer_spec(c.shape) for c in consts],
        out_specs=row(D_MODEL),
        out_shape=jax.ShapeDtypeStruct((t, D_MODEL), F32),
    )(layer, x, od, om, xq, ck, cv, gates, *consts)


def _pack_params(w_in, b_gate, g_mix, diff_q_gain, diff_k_gain, diff_lambda, diff_out_gain, w_diff_o,
                 mla_qa_gain, w_mla_qb, mla_kva_gain, w_mla_kvb, mla_q_gain, mla_k_gain, w_mla_o,
                 mem_gain, w_mem_kv, cross_q_gain, cross_k_gain, w_cross_o, w_out, g_mlp, w_mlp1, w_mlp2):
    nl = w_in.shape[0]
    d = D_MODEL
    half = DIFF_HEAD_DIM // 2
    o_dq, o_dk, o_dv = 0, 1024, 2048
    o_cq = 3072
    o_ckv = o_cq + MLA_Q_LORA
    o_kr = o_ckv + MLA_KV_LORA
    o_xq = o_kr + MLA_ROPE_DIM

    def diff_cols(w):
        r = w.shape[1]
        return w.reshape(nl, r, DIFF_HEADS, 2, 2, half // 2, 2).transpose(0, 1, 2, 4, 3, 6, 5).reshape(
            nl, r, DIFF_HEADS * LANES)

    def diff_gain(g, scale):
        g = g.reshape(nl, 2, 1, half // 2, 2).transpose(0, 1, 2, 4, 3)
        g = jnp.broadcast_to(g, (nl, 2, 2, 2, half // 2)).reshape(nl, 1, LANES)
        return (g * scale).astype(F32)

    rh = MLA_ROPE_DIM // 2

    def mla_head_layout(nope, rope):
        pad = jnp.zeros(nope.shape[:-1] + (LANES - MLA_QK_DIM,), nope.dtype)
        return jnp.concatenate([rope[..., :rh], nope[..., :HALF_LANES - rh], rope[..., rh:],
                                nope[..., HALF_LANES - rh:], pad], axis=-1)

    kr_w = w_in[:, :, o_kr:o_xq]
    zpad = jnp.zeros((nl, d, HALF_LANES - rh), F32)
    kr_cols = jnp.concatenate([kr_w[..., :rh], zpad, kr_w[..., rh:], zpad], axis=-1)
    w_in_p = jnp.concatenate([diff_cols(w_in[:, :, o_dq:o_dk]).astype(BF16),
                              diff_cols(w_in[:, :, o_dk:o_dv]).astype(BF16),
                              w_in[:, :, o_dv:o_kr].astype(BF16), kr_cols.astype(BF16),
                              w_in[:, :, o_xq:o_xq + CROSS_HEADS * CROSS_HEAD_DIM].astype(BF16),
                              (0.5 * w_in[:, :, o_xq + CROSS_HEADS * CROSS_HEAD_DIM:]).astype(BF16)], axis=-1)

    wqb = w_mla_qb.reshape(nl, MLA_Q_LORA, MLA_HEADS, MLA_QK_DIM)
    w_qb_p = mla_head_layout(wqb[..., :MLA_NOPE_DIM], wqb[..., MLA_NOPE_DIM:]).reshape(
        nl, MLA_Q_LORA, MLA_HEADS * LANES).astype(BF16)
    wkvb = w_mla_kvb.reshape(nl, MLA_KV_LORA, MLA_HEADS, MLA_NOPE_DIM + MLA_V_DIM)
    k_nope = wkvb[..., :MLA_NOPE_DIM]
    w_kvk_p = mla_head_layout(k_nope, jnp.zeros(k_nope.shape[:-1] + (MLA_ROPE_DIM,), F32)).reshape(
        nl, MLA_KV_LORA, MLA_HEADS * LANES).astype(BF16)
    w_kvv_p = wkvb[..., MLA_NOPE_DIM:].reshape(nl, MLA_KV_LORA, MLA_HEADS * MLA_V_DIM).astype(BF16)

    def mla_gain(g, scale):
        return (mla_head_layout(g[:, :MLA_NOPE_DIM], g[:, MLA_NOPE_DIM:]) * scale)[:, None, :].astype(F32)

    lam_init = jnp.asarray([0.8 - 0.6 * math.exp(-0.3 * l) for l in range(nl)], F32)
    lam_tab = jnp.zeros((nl, SUBLANES, LANES), F32)
    lam_tab = lam_tab.at[:, 0:4, 0:DIFF_HEAD_DIM].set(diff_lambda.astype(F32))
    lam_tab = lam_tab.at[:, 4, :].set(lam_init[:, None])
    out_gain_rows = jnp.broadcast_to(diff_out_gain[:, :, None], (nl, 2 * DIFF_HEAD_DIM, LANES)).astype(F32)

    row = lambda g: g[:, None, :]
    return dict(
        g_mix=row(g_mix), w_in=w_in_p, b_gate=row(0.5 * b_gate),
        dq_gain=diff_gain(diff_q_gain, DIFF_HEAD_DIM ** -0.5 * LOG2E), dk_gain=diff_gain(diff_k_gain, 1.0),
        qa_gain=row(mla_qa_gain), w_qb=w_qb_p, kva_gain=row(mla_kva_gain),
        w_kvk=w_kvk_p, w_kvv=w_kvv_p,
        mq_gain=mla_gain(mla_q_gain, MLA_QK_DIM ** -0.5 * LOG2E), mk_gain=mla_gain(mla_k_gain, 1.0),
        xq_gain=row(cross_q_gain * (CROSS_HEAD_DIM ** -0.5 * LOG2E)),
        lam_tab=lam_tab, out_gain_rows=out_gain_rows,
        w_diff_o=w_diff_o.astype(BF16), w_mla_o=w_mla_o.astype(BF16),
        mem_gain=row(mem_gain), w_mem_kv=w_mem_kv.astype(BF16), ck_gain=row(cross_k_gain),
        w_cross_o=w_cross_o.astype(BF16), w_out=w_out.astype(BF16),
        g_mlp=row(g_mlp), w_mlp1=w_mlp1.astype(BF16), w_mlp2=w_mlp2.astype(BF16),
    )


def kernel(x, mem, positions, g_mix, w_in, b_gate, diff_q_gain, diff_k_gain, diff_lambda, diff_out_gain,
           w_diff_o, mla_qa_gain, w_mla_qb, mla_kva_gain, w_mla_kvb, mla_q_gain, mla_k_gain, w_mla_o,
           mem_gain, w_mem_kv, cross_q_gain, cross_k_gain, w_cross_o, w_out, g_mlp, w_mlp1, w_mlp2):
    batch, seq, d = x.shape
    tabs = _rope_tables(positions)
    packed = _pack_params(w_in, b_gate, g_mix, diff_q_gain, diff_k_gain, diff_lambda, diff_out_gain,
                          w_diff_o, mla_qa_gain, w_mla_qb, mla_kva_gain, w_mla_kvb, mla_q_gain, mla_k_gain,
                          w_mla_o, mem_gain, w_mem_kv, cross_q_gain, cross_k_gain, w_cross_o, w_out,
                          g_mlp, w_mlp1, w_mlp2)
    xt = x.reshape(batch * seq, d)
    mem2d = mem.reshape(batch * mem.shape[1], d)
    for l in range(w_in.shape[0]):
        layer = jnp.full((1,), l, jnp.int32)
        dq, dk, dv, mq, mk, mv, xq, gates = _inproj(layer, xt, tabs, packed)
        od = _diff_attn(layer, packed, dq, dk, dv, batch, seq)
        om = _mla_attn(mq, mk, mv, batch, seq)
        ck, cv = _mem_kv(layer, mem2d, packed)
        xt = _post(layer, xt, od, om, xq, ck, cv, gates, packed, seq)
    return xt.reshape(batch, seq, d)
```
